```python
import jax, jax.numpy as jnp
from jax import lax
import numpy as np

D_MODEL = 1024
BATCH = 16
SEQ = 4096
DEPTH = 4
DEC_BATCH = 16
DEC_SEQ = 16
PAST_LEN = 4096

CHUNK = 64
N_META = 16
N_MIXERS = 2
N_GLA = (DEPTH + 1) // 2
N_CONV = DEPTH // 2
GLA_HEADS = 4
GLA_DK = D_MODEL // 2 // GLA_HEADS
GLA_DV = D_MODEL // GLA_HEADS
GLA_HK = GLA_HEADS * GLA_DK
GLA_HV = GLA_HEADS * GLA_DV
GLA_GATE_RANK = 16
GLA_TAU = 16.0
GLA_IN = 2 * GLA_HK + 2 * GLA_HV + GLA_GATE_RANK
CONV_W = 3
D_FF = 2816
EPS = 1e-6

kernel_name = "hybrid_gla_shortconv_convffn_stream_step"


def rmsnorm(x, g):
    xf = x.astype(jnp.float32)
    y = xf * lax.rsqrt(jnp.mean(xf * xf, axis=-1, keepdims=True) + EPS) * g.astype(jnp.float32)
    return y.astype(x.dtype)


def causal_dwconv(x, buf, w):
    L = x.shape[1]
    xp = jnp.concatenate([buf.astype(x.dtype), x], axis=1)
    y = xp[:, 0:L] * w[0]
    for j in range(1, CONV_W):
        y = y + xp[:, j:j + L] * w[j]
    return y, xp[:, -(CONV_W - 1):]


def gla_chunk(S, q, k, v, g):
    C = q.shape[1]
    b = jnp.cumsum(g, axis=1)
    o_inter = jnp.einsum('bthd,bhde->bthe', q * jnp.exp(b), S)
    mask = jnp.tril(jnp.ones((C, C), dtype=bool))
    rel = b[:, :, None] - b[:, None, :]
    decay = jnp.exp(jnp.where(mask[None, :, :, None, None], rel, -jnp.inf))
    A = jnp.einsum('bthd,bshd,btshd->bhts', q, k, decay)
    o_intra = jnp.einsum('bhts,bshe->bthe', A, v)
    b_last = b[:, -1]
    k_dec = k * jnp.exp(b_last[:, None] - b)
    S_new = jnp.exp(b_last)[..., None] * S + jnp.einsum('bshd,bshe->bhde', k_dec, v)
    return S_new, o_inter + o_intra


def gla_sequence(S0, q, k, v, g):
    B, L = q.shape[0], q.shape[1]
    lead = L % CHUNK
    S = S0
    outs = []
    if lead:
        S, o = gla_chunk(S, q[:, :lead], k[:, :lead], v[:, :lead], g[:, :lead])
        outs.append(o)
    n = (L - lead) // CHUNK
    if n:
        def to_blocks(a):
            a = a[:, lead:]
            return a.reshape((B, n, CHUNK) + a.shape[2:]).swapaxes(0, 1)

        def step(S_c, blk):
            return gla_chunk(S_c, *blk)

        S, o = lax.scan(step, S, (to_blocks(q), to_blocks(k), to_blocks(v), to_blocks(g)))
        o = o.swapaxes(0, 1).reshape((B, n * CHUNK) + o.shape[3:])
        outs.append(o)
    return S, jnp.concatenate(outs, axis=1)


def gla_mixer(h, S0, w_in, w_gate_up, b_gate, head_gain, w_out):
    B, L, _ = h.shape
    p = h @ w_in
    q, k, v, r, gdown = jnp.split(p, [GLA_HK, 2 * GLA_HK, 2 * GLA_HK + GLA_HV, 2 * GLA_HK + 2 * GLA_HV], axis=-1)
    q = q.reshape(B, L, GLA_HEADS, GLA_DK).astype(jnp.float32) * (GLA_DK ** -0.5)
    k = k.reshape(B, L, GLA_HEADS, GLA_DK).astype(jnp.float32)
    v = v.reshape(B, L, GLA_HEADS, GLA_DV).astype(jnp.float32)
    g = jax.nn.log_sigmoid((gdown @ w_gate_up + b_gate).astype(jnp.float32)) / GLA_TAU
    g = g.reshape(B, L, GLA_HEADS, GLA_DK)
    S, o = gla_sequence(S0.astype(jnp.float32), q, k, v, g)
    o = o * lax.rsqrt(jnp.mean(o * o, axis=-1, keepdims=True) + EPS) * head_gain.astype(jnp.float32)
    o = o.reshape(B, L, GLA_HV).astype(h.dtype) * jax.nn.silu(r)
    return o @ w_out, S.astype(h.dtype)


def shortconv_mixer(h, buf, w_in, conv_w, w_out):
    bg, cg, u = jnp.split(h @ w_in, 3, axis=-1)
    y, new_buf = causal_dwconv(cg * u, buf, conv_w)
    return (bg * y) @ w_out, new_buf


def conv_ffn(h, buf, w_up, conv_w, conv_b, w_down):
    z = h @ w_up
    zc, new_buf = causal_dwconv(z, buf, conv_w)
    gate, up = jnp.split(zc + conv_b, 2, axis=-1)
    return (jax.nn.silu(gate) * up) @ w_down, new_buf


def trunk(x, gla_S, conv_buf, ffn_buf, norm_mix, norm_ffn, norm_final,
          gla_w_in, gla_w_gate_up, gla_b_gate, gla_head_gain, gla_w_out,
          sc_w_in, sc_conv_w, sc_w_out, ffn_w_up, ffn_conv_w, ffn_conv_b, ffn_w_down):
    gla_new, conv_new, ffn_new = [], [], []
    for i in range(DEPTH):
        h = rmsnorm(x, norm_mix[i])
        j = i // N_MIXERS
        if i % N_MIXERS == 0:
            m, s = gla_mixer(h, gla_S[j], gla_w_in[j], gla_w_gate_up[j], gla_b_gate[j],
                             gla_head_gain[j], gla_w_out[j])
            gla_new.append(s)
        else:
            m, s = shortconv_mixer(h, conv_buf[j], sc_w_in[j], sc_conv_w[j], sc_w_out[j])
            conv_new.append(s)
        x = x + m
        f, s = conv_ffn(rmsnorm(x, norm_ffn[i]), ffn_buf[i], ffn_w_up[i], ffn_conv_w[i],
                        ffn_conv_b[i], ffn_w_down[i])
        ffn_new.append(s)
        x = x + f
    return rmsnorm(x, norm_final), jnp.stack(gla_new), jnp.stack(conv_new), jnp.stack(ffn_new)


def setup_inputs(seed: int = 0) -> dict:
    key = jax.random.key(seed)
    ks = jax.random.split(key, 24)

    def nrm(k, shape, scale):
        return jax.random.normal(k, shape, jnp.float32) * scale

    D, F2 = D_MODEL, 2 * D_FF
    return {
        "x_prompt": nrm(ks[0], (BATCH, SEQ, D), 1.0),
        "x_sample": nrm(ks[1], (DEC_BATCH, DEC_SEQ, D), 1.0),
        "state_gla": nrm(ks[2], (N_GLA, DEC_BATCH, GLA_HEADS, GLA_DK, GLA_DV), 1.0),
        "cache_conv": nrm(ks[3], (N_CONV, DEC_BATCH, CONV_W - 1, D), 1.0),
        "cache_ffn": nrm(ks[4], (DEPTH, DEC_BATCH, CONV_W - 1, F2), 1.0),
        "meta": nrm(ks[5], (N_META, D), 1.0),
        "norm_mix": 1.0 + nrm(ks[6], (DEPTH, D), 0.05),
        "norm_ffn": 1.0 + nrm(ks[7], (DEPTH, D), 0.05),
        "norm_final": 1.0 + nrm(ks[8], (D,), 0.05),
        "gla_w_in": nrm(ks[9], (N_GLA, D, GLA_IN), D ** -0.5),
        "gla_w_gate_up": nrm(ks[10], (N_GLA, GLA_GATE_RANK, GLA_HK), GLA_GATE_RANK ** -0.5),
        "gla_b_gate": nrm(ks[11], (N_GLA, GLA_HK), 0.1),
        "gla_head_gain": 1.0 + nrm(ks[12], (N_GLA, GLA_HEADS, GLA_DV), 0.05),
        "gla_w_out": nrm(ks[13], (N_GLA, GLA_HV, D), GLA_HV ** -0.5),
        "sc_w_in": nrm(ks[14], (N_CONV, D, 3 * D), D ** -0.5),
        "sc_conv_w": nrm(ks[15], (N_CONV, CONV_W, D), CONV_W ** -0.5),
        "sc_w_out": nrm(ks[16], (N_CONV, D, D), D ** -0.5),
        "ffn_w_up": nrm(ks[17], (DEPTH, D, F2), D ** -0.5),
        "ffn_conv_w": nrm(ks[18], (DEPTH, CONV_W, F2), CONV_W ** -0.5),
        "ffn_conv_b": nrm(ks[19], (DEPTH, F2), 0.01),
        "ffn_w_down": nrm(ks[20], (DEPTH, D_FF, D), D_FF ** -0.5),
    }


def reference(x_prompt, x_sample, state_gla, cache_conv, cache_ffn, meta, norm_mix, norm_ffn, norm_final,
              gla_w_in, gla_w_gate_up, gla_b_gate, gla_head_gain, gla_w_out,
              sc_w_in, sc_conv_w, sc_w_out, ffn_w_up, ffn_conv_w, ffn_conv_b, ffn_w_down):
    weights = (norm_mix, norm_ffn, norm_final, gla_w_in, gla_w_gate_up, gla_b_gate, gla_head_gain,
               gla_w_out, sc_w_in, sc_conv_w, sc_w_out, ffn_w_up, ffn_conv_w, ffn_conv_b, ffn_w_down)
    dt = x_prompt.dtype
    B = x_prompt.shape[0]
    xp = jnp.concatenate([jnp.broadcast_to(meta.astype(dt)[None], (B, N_META, D_MODEL)), x_prompt], axis=1)
    S0 = jnp.zeros((N_GLA, B, GLA_HEADS, GLA_DK, GLA_DV), dt)
    c0 = jnp.zeros((N_CONV, B, CONV_W - 1, D_MODEL), dt)
    f0 = jnp.zeros((DEPTH, B, CONV_W - 1, 2 * D_FF), dt)
    yp, gla_p, conv_p, ffn_p = trunk(xp, S0, c0, f0, *weights)
    y_prompt = yp[:, N_META:]
    y_sample, gla_s, conv_s, ffn_s = trunk(x_sample, state_gla, cache_conv, cache_ffn, *weights)
    return (y_prompt, y_sample, gla_p, gla_s, conv_p, conv_s, ffn_p, ffn_s)
```

```python
import functools

import jax
import jax.numpy as jnp
from jax import lax
from jax.experimental import pallas as pl
from jax.experimental.pallas import tpu as pltpu

N_META = 16
GLA_HEADS = 4
GLA_GATE_RANK = 16
GLA_TAU = 16.0
CONV_W = 3
EPS = 1e-6
GLA_CHUNK = 64
FFN_COLS = 256
LANES = 128
SUBLANES = 8
VMEM_LIMIT_BYTES = 56 * 1024 * 1024

_BF16 = jnp.bfloat16
_F32 = jnp.float32


def _dot(a, b):
    return jnp.dot(a, b, preferred_element_type=_F32)


def _dot_nt(a, b):
    return lax.dot_general(a, b, (((1,), (1,)), ((), ())), preferred_element_type=_F32)


def _dot_tn(a, b):
    return lax.dot_general(a, b, (((0,), (0,)), ((), ())), preferred_element_type=_F32)


def _rmsnorm(x, g):
    return x * lax.rsqrt(jnp.mean(x * x, axis=-1, keepdims=True) + EPS) * g


def _silu(x):
    return x / (1.0 + jnp.exp(-x))


def _shift_rows(z, tail, k):
    rolled = pltpu.roll(z, k, 0)
    head = rolled[:SUBLANES]
    rid = lax.broadcasted_iota(jnp.int32, head.shape, 0)
    head = jnp.where(rid < k, pltpu.roll(tail, k, 0), head)
    if z.shape[0] == SUBLANES:
        return head
    return jnp.concatenate([head, rolled[SUBLANES:]], axis=0)


def _causal_conv(z, tail, w):
    return w[2:3] * z + w[1:2] * _shift_rows(z, tail, 1) + w[0:1] * _shift_rows(z, tail, 2)


def _ffn_kernel(x_ref, init_ref, gn_ref, wg_ref, wu_ref, cwg_ref, cwu_ref, cbg_ref, cbu_ref, wd_ref,
                gfin_ref, o_ref, cache_ref, tail_g, tail_u, *, final_norm):
    t = pl.program_id(1)
    n_t = pl.num_programs(1)
    n_c = wg_ref.shape[0]
    tm = x_ref.shape[1]

    @pl.when(t == 0)
    def _():
        tail_g[...] = init_ref[0, 0]
        tail_u[...] = init_ref[0, 1]

    x = x_ref[0]
    h = _rmsnorm(x, gn_ref[...]).astype(_BF16)
    acc = x
    for j in range(n_c):
        zg = _dot(h, wg_ref[j])
        zu = _dot(h, wu_ref[j])
        cg = _causal_conv(zg, tail_g[j], cwg_ref[j]) + cbg_ref[j]
        cu = _causal_conv(zu, tail_u[j], cwu_ref[j]) + cbu_ref[j]
        tail_g[j] = zg[tm - SUBLANES:]
        tail_u[j] = zu[tm - SUBLANES:]
        a = (_silu(cg) * cu).astype(_BF16)
        acc = acc + _dot(a, wd_ref[j])
    if final_norm:
        acc = _rmsnorm(acc, gfin_ref[...])
    o_ref[0] = acc

    @pl.when(t == n_t - 1)
    def _():
        cache_ref[0, 0] = tail_g[...]
        cache_ref[0, 1] = tail_u[...]


def _const_spec(shape):
    nd = len(shape)
    return pl.BlockSpec(shape, lambda b, t: (0,) * nd, pipeline_mode=pl.Buffered(1))


def _init_spec(shape, shared):
    nd = len(shape)
    block = (1,) + tuple(shape[1:])
    if shared:
        return pl.BlockSpec(block, lambda b, t: (0,) * nd)
    return pl.BlockSpec(block, lambda b, t: (b,) + (0,) * (nd - 1))


def _compiler_params():
    return pltpu.CompilerParams(dimension_semantics=("arbitrary", "arbitrary"),
                                vmem_limit_bytes=VMEM_LIMIT_BYTES)


def _ffn_call(x, init, w, gfin, *, n_seq, tm, final_norm):
    _, seq, d = x.shape
    n_c = w["wg"].shape[0]
    fc = w["wg"].shape[2]
    shared = init.shape[0] == 1 and n_seq > 1
    tail_shape = (n_c, SUBLANES, fc)
    out, cache = pl.pallas_call(
        functools.partial(_ffn_kernel, final_norm=final_norm),
        grid=(n_seq, seq // tm),
        in_specs=[
            pl.BlockSpec((1, tm, d), lambda b, t: (b, t, 0)),
            _init_spec(init.shape, shared),
            _const_spec(w["gn"].shape),
            _const_spec(w["wg"].shape), _const_spec(w["wu"].shape),
            _const_spec(w["cwg"].shape), _const_spec(w["cwu"].shape),
            _const_spec(w["cbg"].shape), _const_spec(w["cbu"].shape),
            _const_spec(w["wd"].shape),
            _const_spec(gfin.shape),
        ],
        out_specs=[
            pl.BlockSpec((1, tm, d), lambda b, t: (b, t, 0)),
            pl.BlockSpec((1, 2) + tail_shape, lambda b, t: (b, 0, 0, 0, 0)),
        ],
        out_shape=[
            jax.ShapeDtypeStruct((n_seq, seq, d), _F32),
            jax.ShapeDtypeStruct((n_seq, 2) + tail_shape, _F32),
        ],
        scratch_shapes=[pltpu.VMEM(tail_shape, _F32), pltpu.VMEM(tail_shape, _F32)],
        compiler_params=_compiler_params(),
        name="conv_ffn",
    )(x, init, w["gn"], w["wg"], w["wu"], w["cwg"], w["cwu"], w["cbg"], w["cbu"], w["wd"], gfin)
    return out, cache


def _sc_kernel(x_ref, init_ref, gn_ref, win_ref, cw_ref, wout_ref, o_ref, cache_ref, tail):
    t = pl.program_id(1)
    n_t = pl.num_programs(1)
    tm = x_ref.shape[1]

    @pl.when(t == 0)
    def _():
        tail[...] = init_ref[0]

    x = x_ref[0]
    h = _rmsnorm(x, gn_ref[...]).astype(_BF16)
    bg = _dot(h, win_ref[0])
    ci = _dot(h, win_ref[1]) * _dot(h, win_ref[2])
    y = _causal_conv(ci, tail[...], cw_ref[...])
    tail[...] = ci[tm - SUBLANES:]
    o_ref[0] = x + _dot((bg * y).astype(_BF16), wout_ref[...])

    @pl.when(t == n_t - 1)
    def _():
        cache_ref[0] = tail[...]


def _sc_call(x, init, w, *, n_seq, tm):
    _, seq, d = x.shape
    shared = init.shape[0] == 1 and n_seq > 1
    out, cache = pl.pallas_call(
        _sc_kernel,
        grid=(n_seq, seq // tm),
        in_specs=[
            pl.BlockSpec((1, tm, d), lambda b, t: (b, t, 0)),
            _init_spec(init.shape, shared),
            _const_spec(w["gn"].shape), _const_spec(w["win"].shape),
            _const_spec(w["cw"].shape), _const_spec(w["wout"].shape),
        ],
        out_specs=[
            pl.BlockSpec((1, tm, d), lambda b, t: (b, t, 0)),
            pl.BlockSpec((1, SUBLANES, d), lambda b, t: (b, 0, 0)),
        ],
        out_shape=[
            jax.ShapeDtypeStruct((n_seq, seq, d), _F32),
            jax.ShapeDtypeStruct((n_seq, SUBLANES, d), _F32),
        ],
        scratch_shapes=[pltpu.VMEM((SUBLANES, d), _F32)],
        compiler_params=_compiler_params(),
        name="shortconv_mixer",
    )(x, init, w["gn"], w["win"], w["cw"], w["wout"])
    return out, cache


def _cumsum_rows(a):
    n = a.shape[0]
    rid = lax.broadcasted_iota(jnp.int32, a.shape, 0)
    s = 1
    while s < n:
        a = a + jnp.where(rid >= s, pltpu.roll(a, s, 0), 0.0)
        s *= 2
    return a


def _gla_chunk(q, k, v, g, s_ref, hg, c):
    n_h = GLA_HEADS
    dk = q.shape[1] // n_h
    dv = v.shape[1] // n_h
    b = _cumsum_rows(g)
    b_last = b[c - 1:c]
    qe = (q * jnp.exp(b)).astype(_BF16)
    kd = (k * jnp.exp(b_last - b)).astype(_BF16)
    decay = jnp.exp(b_last)

    rid = lax.broadcasted_iota(jnp.int32, b.shape, 0)
    tt = lax.broadcasted_iota(jnp.int32, (c, c), 0)
    ss = lax.broadcasted_iota(jnp.int32, (c, c), 1)
    txs = tt ^ ss
    lower = tt > ss

    levels = [(tt == ss, q.astype(_BF16), k.astype(_BF16))]
    e_k = b
    half = 1
    while half < c:
        hi = (rid & half) != 0
        b_mid = jnp.where(hi, pltpu.roll(e_k, half, 0), e_k)
        scale = jnp.exp(-jnp.abs(b - b_mid))
        mask = lower & (txs >= half) & (txs < 2 * half)
        levels.append((mask, (q * scale).astype(_BF16), (k * scale).astype(_BF16)))
        if 2 * half < c:
            e_k = jnp.where(hi, e_k, pltpu.roll(e_k, c - half, 0))
        half *= 2

    outs = []
    for hh in range(n_h):
        ks = slice(hh * dk, (hh + 1) * dk)
        vs = slice(hh * dv, (hh + 1) * dv)
        a = jnp.zeros((c, c), _F32)
        for mask, qs, kk in levels:
            a = jnp.where(mask, _dot_nt(qs[:, ks], kk[:, ks]), a)
        s_old = s_ref[hh]
        o = _dot(qe[:, ks], s_old.astype(_BF16)) + _dot(a.astype(_BF16), v[:, vs])
        o = o * lax.rsqrt(jnp.mean(o * o, axis=-1, keepdims=True) + EPS) * hg[:, vs]
        outs.append(o)
        dcol = jnp.transpose(jnp.broadcast_to(decay[:, ks], (dk, dk)))
        dmat = jnp.concatenate([dcol] * (dv // dk), axis=1)
        s_ref[hh] = dmat * s_old + _dot_tn(kd[:, ks], v[:, vs])
    return jnp.concatenate(outs, axis=1)


def _gla_kernel(x_ref, s0_ref, gn_ref, win_ref, wgd_ref, wgu_ref, bgate_ref, hg_ref, wout_ref,
                o_ref, sout_ref, s_scr, p_scr, g_scr, o_scr, *, chunk):
    t = pl.program_id(1)
    n_t = pl.num_programs(1)
    tm = x_ref.shape[1]
    hk = g_scr.shape[1]
    hv = o_scr.shape[1]

    @pl.when(t == 0)
    def _():
        s_scr[...] = s0_ref[0]

    x = x_ref[0]
    h = _rmsnorm(x, gn_ref[...]).astype(_BF16)
    p_scr[...] = _dot(h, win_ref[...])
    gdown = _dot(h, wgd_ref[...]).astype(_BF16)
    z = _dot(gdown, wgu_ref[...]) + bgate_ref[...]
    g_scr[...] = (jnp.minimum(z, 0.0) - jnp.log1p(jnp.exp(-jnp.abs(z)))) * (1.0 / GLA_TAU)
    q_scale = float(hk // GLA_HEADS) ** -0.5

    def body(ci, carry):
        r0 = pl.multiple_of(ci * chunk, chunk)
        rows = pl.ds(r0, chunk)
        q = p_scr[rows, 0:hk] * q_scale
        k = p_scr[rows, hk:2 * hk]
        v = p_scr[rows, 2 * hk:2 * hk + hv].astype(_BF16)
        o_scr[rows, :] = _gla_chunk(q, k, v, g_scr[rows, :], s_scr, hg_ref[...], chunk)
        return carry

    lax.fori_loop(0, tm // chunk, body, 0)

    r = p_scr[:, 2 * hk + hv:2 * hk + 2 * hv]
    o_ref[0] = x + _dot((o_scr[...] * _silu(r)).astype(_BF16), wout_ref[...])

    @pl.when(t == n_t - 1)
    def _():
        sout_ref[0] = s_scr[...]


def _gla_call(x, s0, w, *, n_seq, tm, chunk):
    _, seq, d = x.shape
    n_h, dk, dv = s0.shape[1:]
    hk, hv = n_h * dk, n_h * dv
    shared = s0.shape[0] == 1 and n_seq > 1
    out, s_out = pl.pallas_call(
        functools.partial(_gla_kernel, chunk=chunk),
        grid=(n_seq, seq // tm),
        in_specs=[
            pl.BlockSpec((1, tm, d), lambda b, t: (b, t, 0)),
            _init_spec(s0.shape, shared),
            _const_spec(w["gn"].shape), _const_spec(w["win"].shape),
            _const_spec(w["wgd"].shape), _const_spec(w["wgu"].shape),
            _const_spec(w["bgate"].shape), _const_spec(w["hg"].shape),
            _const_spec(w["wout"].shape),
        ],
        out_specs=[
            pl.BlockSpec((1, tm, d), lambda b, t: (b, t, 0)),
            pl.BlockSpec((1, n_h, dk, dv), lambda b, t: (b, 0, 0, 0)),
        ],
        out_shape=[
            jax.ShapeDtypeStruct((n_seq, seq, d), _F32),
            jax.ShapeDtypeStruct((n_seq, n_h, dk, dv), _F32),
        ],
        scratch_shapes=[
            pltpu.VMEM((n_h, dk, dv), _F32),
            pltpu.VMEM((tm, 2 * hk + 2 * hv), _F32),
            pltpu.VMEM((tm, hk), _F32),
            pltpu.VMEM((tm, hv), _F32),
        ],
        compiler_params=_compiler_params(),
        name="gla_mixer",
    )(x, s0, w["gn"], w["win"], w["wgd"], w["wgu"], w["bgate"], w["hg"], w["wout"])
    return out, s_out


def _row(v):
    return v.reshape(1, -1).astype(_F32)


def _prep_gla(norm, w_in, w_gate_up, b_gate, head_gain, w_out):
    hk = w_gate_up.shape[1]
    main = w_in.shape[1] - GLA_GATE_RANK
    wgd = jnp.pad(w_in[:, main:], ((0, 0), (0, LANES - GLA_GATE_RANK)))
    wgu = jnp.pad(w_gate_up, ((0, LANES - GLA_GATE_RANK), (0, 0)))
    return dict(gn=_row(norm), win=w_in[:, :main].astype(_BF16), wgd=wgd.astype(_BF16),
                wgu=wgu.astype(_BF16), bgate=_row(b_gate), hg=_row(head_gain),
                wout=w_out.astype(_BF16))


def _prep_sc(norm, w_in, conv_w, w_out):
    d = w_in.shape[0]
    win = w_in.reshape(d, 3, d).transpose(1, 0, 2).astype(_BF16)
    return dict(gn=_row(norm), win=win, cw=conv_w.astype(_F32), wout=w_out.astype(_BF16))


def _prep_ffn(norm, w_up, conv_w, conv_b, w_down):
    d, f2 = w_up.shape
    f = f2 // 2
    fc = FFN_COLS
    n_c = f // fc

    def cols(a):
        return a.reshape(a.shape[0], n_c, fc).transpose(1, 0, 2)

    return dict(gn=_row(norm),
                wg=cols(w_up[:, :f]).astype(_BF16), wu=cols(w_up[:, f:]).astype(_BF16),
                cwg=cols(conv_w[:, :f]).astype(_F32), cwu=cols(conv_w[:, f:]).astype(_F32),
                cbg=cols(conv_b[None, :f]).astype(_F32), cbu=cols(conv_b[None, f:]).astype(_F32),
                wd=w_down.reshape(n_c, fc, d).astype(_BF16))


def _tail_from_cache(cache):
    return jnp.pad(cache, ((0, 0), (SUBLANES - (CONV_W - 1), 0), (0, 0)))


def _ffn_tail_from_cache(cache, n_c, fc):
    b = cache.shape[0]
    t = _tail_from_cache(cache).reshape(b, SUBLANES, 2, n_c, fc)
    return t.transpose(0, 2, 3, 1, 4)


def _ffn_cache_from_tail(tail):
    b = tail.shape[0]
    rows = tail[:, :, :, SUBLANES - (CONV_W - 1):, :]
    return rows.transpose(0, 3, 1, 2, 4).reshape(b, CONV_W - 1, -1)


def _trunk(x, gla_s, conv_tail, ffn_tail, params, gfin, *, tm_mix, tm_ffn, chunk):
    n_seq = x.shape[0]
    depth = len(params)
    gla_new, conv_new, ffn_new = [], [], []
    i_gla = i_conv = 0
    for i, (mix, ffn) in enumerate(params):
        if i % 2 == 0:
            x, s = _gla_call(x, gla_s[i_gla], mix, n_seq=n_seq, tm=tm_mix, chunk=chunk)
            gla_new.append(s)
            i_gla += 1
        else:
            x, s = _sc_call(x, conv_tail[i_conv], mix, n_seq=n_seq, tm=tm_ffn)
            conv_new.append(s)
            i_conv += 1
        x, s = _ffn_call(x, ffn_tail[i], ffn, gfin, n_seq=n_seq, tm=tm_ffn,
                         final_norm=(i == depth - 1))
        ffn_new.append(s)
    return x, gla_new, conv_new, ffn_new


def kernel(x_prompt, x_sample, state_gla, cache_conv, cache_ffn, meta, norm_mix, norm_ffn, norm_final,
           gla_w_in, gla_w_gate_up, gla_b_gate, gla_head_gain, gla_w_out,
           sc_w_in, sc_conv_w, sc_w_out, ffn_w_up, ffn_conv_w, ffn_conv_b, ffn_w_down):
    depth = norm_mix.shape[0]
    n_b, seq, d = x_prompt.shape
    n_s, s_len, _ = x_sample.shape
    assert s_len == N_META and seq % GLA_CHUNK == 0
    f = ffn_w_down.shape[1]
    n_c, fc = f // FFN_COLS, FFN_COLS

    params = []
    for i in range(depth):
        j = i // 2
        if i % 2 == 0:
            mix = _prep_gla(norm_mix[i], gla_w_in[j], gla_w_gate_up[j], gla_b_gate[j],
                            gla_head_gain[j], gla_w_out[j])
        else:
            mix = _prep_sc(norm_mix[i], sc_w_in[j], sc_conv_w[j], sc_w_out[j])
        params.append((mix, _prep_ffn(norm_ffn[i], ffn_w_up[i], ffn_conv_w[i], ffn_conv_b[i],
                                      ffn_w_down[i])))
    gfin = _row(norm_final)

    xs = jnp.concatenate([x_sample, meta[None].astype(x_sample.dtype)], axis=0)
    zero1 = lambda a: jnp.zeros((a.shape[0], 1) + a.shape[2:], a.dtype)
    gla_s = jnp.concatenate([state_gla, zero1(state_gla)], axis=1).astype(_F32)
    conv_c = jnp.concatenate([cache_conv, zero1(cache_conv)], axis=1).astype(_F32)
    ffn_c = jnp.concatenate([cache_ffn, zero1(cache_ffn)], axis=1).astype(_F32)
    ys, gla_1, conv_1, ffn_1 = _trunk(
        xs, [gla_s[j] for j in range(gla_s.shape[0])],
        [_tail_from_cache(conv_c[j]) for j in range(conv_c.shape[0])],
        [_ffn_tail_from_cache(ffn_c[i], n_c, fc) for i in range(depth)],
        params, gfin, tm_mix=s_len, tm_ffn=s_len, chunk=s_len)

    yp, gla_2, conv_2, ffn_2 = _trunk(
        x_prompt, [s[n_s:] for s in gla_1], [s[n_s:] for s in conv_1], [s[n_s:] for s in ffn_1],
        params, gfin, tm_mix=256, tm_ffn=256, chunk=GLA_CHUNK)

    dt = x_prompt.dtype
    tail2 = lambda s: s[:, SUBLANES - (CONV_W - 1):, :]
    return (yp.astype(dt), ys[:n_s].astype(dt),
            jnp.stack(gla_2).astype(dt), jnp.stack([s[:n_s] for s in gla_1]).astype(dt),
            jnp.stack([tail2(s) for s in conv_2]).astype(dt),
            jnp.stack([tail2(s[:n_s]) for s in conv_1]).astype(dt),
            jnp.stack([_ffn_cache_from_tail(s) for s in ffn_2]).astype(dt),
            jnp.stack([_ffn_cache_from_tail(s[:n_s]) for s in ffn_1]).astype(dt))
```

```python
import functools

import jax
import jax.numpy as jnp
from jax import lax
from jax.experimental import pallas as pl
from jax.experimental.pallas import tpu as pltpu

N_META = 16
GLA_HEADS = 4
GLA_GATE_RANK = 16
GLA_TAU = 16.0
CONV_W = 3
EPS = 1e-6
GLA_CHUNK = 64
FFN_COLS = 256
FFN_ROW_BLOCK = 512
TILE_GLA = 256
TILE_SC = 256
TILE_FFN = 512
LANES = 128
SUBLANES = 8
VMEM_LIMIT_BYTES = 56 * 1024 * 1024

_BF16 = jnp.bfloat16
_F32 = jnp.float32


def _dot(a, b):
    return jnp.dot(a, b, preferred_element_type=_F32)


def _dot_nt(a, b):
    return lax.dot_general(a, b, (((1,), (1,)), ((), ())), preferred_element_type=_F32)


def _dot_tn(a, b):
    return lax.dot_general(a, b, (((0,), (0,)), ((), ())), preferred_element_type=_F32)


def _rmsnorm(x, g):
    return x * lax.rsqrt(jnp.mean(x * x, axis=-1, keepdims=True) + EPS) * g


_NEG_LOG2E = -1.4426950408889634


def _silu(x):
    return x / (1.0 + jnp.exp2(x * _NEG_LOG2E))


def _shift_rows(z, tail, k):
    rolled = pltpu.roll(z, k, 0)
    head = rolled[:SUBLANES]
    rid = lax.broadcasted_iota(jnp.int32, head.shape, 0)
    head = jnp.where(rid < k, pltpu.roll(tail, k, 0), head)
    if z.shape[0] == SUBLANES:
        return head
    return jnp.concatenate([head, rolled[SUBLANES:]], axis=0)


def _causal_conv(z, tail, w):
    return w[2:3] * z + w[1:2] * _shift_rows(z, tail, 1) + w[0:1] * _shift_rows(z, tail, 2)


def _ffn_kernel(x_ref, init_ref, gn_ref, wg_ref, wu_ref, cwg_ref, cwu_ref, cbg_ref, cbu_ref, wd_ref,
                gfin_ref, o_ref, cache_ref, zg_buf, zu_buf, a_buf, *, final_norm, mb):
    t = pl.program_id(1)
    n_t = pl.num_programs(1)
    n_c = wg_ref.shape[0]
    tm = x_ref.shape[1]
    s8 = SUBLANES

    @pl.when(t == 0)
    def _():
        zg_buf[:, 0:s8, :] = init_ref[0, 0]
        zu_buf[:, 0:s8, :] = init_ref[0, 1]

    n_r = tm // mb
    units = [(r, j) for r in range(n_r) for j in range(n_c)]
    h_blocks = {}

    def h_of(r):
        if r not in h_blocks:
            h_blocks[r] = _rmsnorm(x_ref[0, r * mb:(r + 1) * mb, :], gn_ref[...]).astype(_BF16)
        return h_blocks[r]

    def up(u):
        r, j = u
        rows = slice(s8 + r * mb, s8 + (r + 1) * mb)
        zg_buf[j, rows, :] = _dot(h_of(r), wg_ref[j])
        zu_buf[j, rows, :] = _dot(h_of(r), wu_ref[j])

    def conv(buf, cw_ref, cb_ref, u):
        r, j = u
        w = cw_ref[j]
        r0 = s8 + r * mb
        return (w[2:3] * buf[j, r0:r0 + mb, :] + w[1:2] * buf[j, r0 - 1:r0 - 1 + mb, :]
                + w[0:1] * buf[j, r0 - 2:r0 - 2 + mb, :] + cb_ref[j])

    def act(u):
        r, j = u
        a = _silu(conv(zg_buf, cwg_ref, cbg_ref, u)) * conv(zu_buf, cwu_ref, cbu_ref, u)
        if r == n_r - 1:
            zg_buf[j, 0:s8, :] = zg_buf[j, tm:tm + s8, :]
            zu_buf[j, 0:s8, :] = zu_buf[j, tm:tm + s8, :]
        a_buf[j, r * mb:(r + 1) * mb, :] = a.astype(_BF16)

    acc = {}

    def down(u):
        r, j = u
        if j == 0:
            acc[r] = x_ref[0, r * mb:(r + 1) * mb, :]
        acc[r] = acc[r] + _dot(a_buf[j, r * mb:(r + 1) * mb, :], wd_ref[j])
        if j == n_c - 1:
            y = acc.pop(r)
            if final_norm:
                y = _rmsnorm(y, gfin_ref[...])
            o_ref[0, r * mb:(r + 1) * mb, :] = y

    up(units[0])
    if len(units) > 1:
        up(units[1])
    act(units[0])
    for i in range(1, len(units)):
        act(units[i])
        if i + 1 < len(units):
            up(units[i + 1])
        down(units[i - 1])
    down(units[-1])

    @pl.when(t == n_t - 1)
    def _():
        cache_ref[0, 0] = zg_buf[:, 0:s8, :]
        cache_ref[0, 1] = zu_buf[:, 0:s8, :]


def _const_spec(shape):
    nd = len(shape)
    return pl.BlockSpec(shape, lambda b, t: (0,) * nd, pipeline_mode=pl.Buffered(1))


def _init_spec(shape, shared):
    nd = len(shape)
    block = (1,) + tuple(shape[1:])
    if shared:
        return pl.BlockSpec(block, lambda b, t: (0,) * nd)
    return pl.BlockSpec(block, lambda b, t: (b,) + (0,) * (nd - 1))


def _compiler_params():
    return pltpu.CompilerParams(dimension_semantics=("arbitrary", "arbitrary"),
                                vmem_limit_bytes=VMEM_LIMIT_BYTES)


def _ffn_call(x, init, w, gfin, *, n_seq, tm, final_norm):
    _, seq, d = x.shape
    n_c = w["wg"].shape[0]
    fc = w["wg"].shape[2]
    shared = init.shape[0] == 1 and n_seq > 1
    tail_shape = (n_c, SUBLANES, fc)
    out, cache = pl.pallas_call(
        functools.partial(_ffn_kernel, final_norm=final_norm, mb=min(tm, FFN_ROW_BLOCK)),
        grid=(n_seq, seq // tm),
        in_specs=[
            pl.BlockSpec((1, tm, d), lambda b, t: (b, t, 0)),
            _init_spec(init.shape, shared),
            _const_spec(w["gn"].shape),
            _const_spec(w["wg"].shape), _const_spec(w["wu"].shape),
            _const_spec(w["cwg"].shape), _const_spec(w["cwu"].shape),
            _const_spec(w["cbg"].shape), _const_spec(w["cbu"].shape),
            _const_spec(w["wd"].shape),
            _const_spec(gfin.shape),
        ],
        out_specs=[
            pl.BlockSpec((1, tm, d), lambda b, t: (b, t, 0)),
            pl.BlockSpec((1, 2) + tail_shape, lambda b, t: (b, 0, 0, 0, 0)),
        ],
        out_shape=[
            jax.ShapeDtypeStruct((n_seq, seq, d), _F32),
            jax.ShapeDtypeStruct((n_seq, 2) + tail_shape, _F32),
        ],
        scratch_shapes=[pltpu.VMEM((n_c, SUBLANES + tm, fc), _F32),
                        pltpu.VMEM((n_c, SUBLANES + tm, fc), _F32),
                        pltpu.VMEM((n_c, tm, fc), _BF16)],
        compiler_params=_compiler_params(),
        name="conv_ffn",
    )(x, init, w["gn"], w["wg"], w["wu"], w["cwg"], w["cwu"], w["cbg"], w["cbu"], w["wd"], gfin)
    return out, cache


def _sc_kernel(x_ref, init_ref, gn_ref, win_ref, cw_ref, wout_ref, o_ref, cache_ref, tail):
    t = pl.program_id(1)
    n_t = pl.num_programs(1)
    tm = x_ref.shape[1]

    @pl.when(t == 0)
    def _():
        tail[...] = init_ref[0]

    x = x_ref[0]
    h = _rmsnorm(x, gn_ref[...]).astype(_BF16)
    bg = _dot(h, win_ref[0])
    ci = _dot(h, win_ref[1]) * _dot(h, win_ref[2])
    y = _causal_conv(ci, tail[...], cw_ref[...])
    tail[...] = ci[tm - SUBLANES:]
    o_ref[0] = x + _dot((bg * y).astype(_BF16), wout_ref[...])

    @pl.when(t == n_t - 1)
    def _():
        cache_ref[0] = tail[...]


def _sc_call(x, init, w, *, n_seq, tm):
    _, seq, d = x.shape
    shared = init.shape[0] == 1 and n_seq > 1
    out, cache = pl.pallas_call(
        _sc_kernel,
        grid=(n_seq, seq // tm),
        in_specs=[
            pl.BlockSpec((1, tm, d), lambda b, t: (b, t, 0)),
            _init_spec(init.shape, shared),
            _const_spec(w["gn"].shape), _const_spec(w["win"].shape),
            _const_spec(w["cw"].shape), _const_spec(w["wout"].shape),
        ],
        out_specs=[
            pl.BlockSpec((1, tm, d), lambda b, t: (b, t, 0)),
            pl.BlockSpec((1, SUBLANES, d), lambda b, t: (b, 0, 0)),
        ],
        out_shape=[
            jax.ShapeDtypeStruct((n_seq, seq, d), _F32),
            jax.ShapeDtypeStruct((n_seq, SUBLANES, d), _F32),
        ],
        scratch_shapes=[pltpu.VMEM((SUBLANES, d), _F32)],
        compiler_params=_compiler_params(),
        name="shortconv_mixer",
    )(x, init, w["gn"], w["win"], w["cw"], w["wout"])
    return out, cache


def _cumsum_rows(a):
    n = a.shape[0]
    rid = lax.broadcasted_iota(jnp.int32, a.shape, 0)
    s = 1
    while s < n:
        a = a + jnp.where(rid >= s, pltpu.roll(a, s, 0), 0.0)
        s *= 2
    return a


def _gla_chunk_scores(q, k, v, g, c):
    n_h = GLA_HEADS
    dk = q.shape[1] // n_h
    dv = v.shape[1] // n_h
    b = _cumsum_rows(g)
    b_last = b[c - 1:c]
    qe = (q * jnp.exp(b)).astype(_BF16)
    kd = (k * jnp.exp(b_last - b)).astype(_BF16)
    decay = jnp.exp(b_last)

    rid = lax.broadcasted_iota(jnp.int32, b.shape, 0)
    tt = lax.broadcasted_iota(jnp.int32, (c, c), 0)
    ss = lax.broadcasted_iota(jnp.int32, (c, c), 1)
    txs = tt ^ ss
    lower = tt > ss

    levels = [(tt == ss, q.astype(_BF16), k.astype(_BF16))]
    e_k = b
    half = 1
    while half < c:
        hi = (rid & half) != 0
        b_mid = jnp.where(hi, pltpu.roll(e_k, half, 0), e_k)
        scale = jnp.exp(-jnp.abs(b - b_mid))
        mask = lower & (txs >= half) & (txs < 2 * half)
        levels.append((mask, (q * scale).astype(_BF16), (k * scale).astype(_BF16)))
        if 2 * half < c:
            e_k = jnp.where(hi, e_k, pltpu.roll(e_k, c - half, 0))
        half *= 2

    scores = []
    for hh in range(n_h):
        ks = slice(hh * dk, (hh + 1) * dk)
        a = jnp.zeros((c, c), _F32)
        for mask, qs, kk in levels:
            a = jnp.where(mask, _dot_nt(qs[:, ks], kk[:, ks]), a)
        scores.append(a.astype(_BF16))
    return dict(scores=scores, qe=qe, kd=kd, decay=decay, v=v)


def _gla_chunk_apply(pre, s_ref, hg):
    n_h = GLA_HEADS
    qe, kd, decay, v = pre["qe"], pre["kd"], pre["decay"], pre["v"]
    dk = qe.shape[1] // n_h
    dv = v.shape[1] // n_h
    outs = []
    for hh in range(n_h):
        ks = slice(hh * dk, (hh + 1) * dk)
        vs = slice(hh * dv, (hh + 1) * dv)
        s_old = s_ref[hh]
        o = _dot(qe[:, ks], s_old.astype(_BF16)) + _dot(pre["scores"][hh], v[:, vs])
        o = o * lax.rsqrt(jnp.mean(o * o, axis=-1, keepdims=True) + EPS) * hg[:, vs]
        outs.append(o)
        dcol = jnp.transpose(jnp.broadcast_to(decay[:, ks], (dk, dk)))
        dmat = jnp.concatenate([dcol] * (dv // dk), axis=1)
        s_ref[hh] = dmat * s_old + _dot_tn(kd[:, ks], v[:, vs])
    return jnp.concatenate(outs, axis=1)


def _gla_kernel(x_ref, s0_ref, gn_ref, win_ref, wgd_ref, wgu_ref, bgate_ref, hg_ref, wout_ref,
                o_ref, sout_ref, s_scr, p_scr, g_scr, o_scr, *, chunk):
    t = pl.program_id(1)
    n_t = pl.num_programs(1)
    tm = x_ref.shape[1]
    hk = g_scr.shape[1]
    hv = o_scr.shape[1]

    @pl.when(t == 0)
    def _():
        s_scr[...] = s0_ref[0]

    x = x_ref[0]
    h = _rmsnorm(x, gn_ref[...]).astype(_BF16)
    p_scr[...] = _dot(h, win_ref[...])
    gdown = _dot(h, wgd_ref[...]).astype(_BF16)
    z = _dot(gdown, wgu_ref[...]) + bgate_ref[...]
    g_scr[...] = (jnp.minimum(z, 0.0) - jnp.log1p(jnp.exp(-jnp.abs(z)))) * (1.0 / GLA_TAU)
    q_scale = float(hk // GLA_HEADS) ** -0.5

    n_chunks = tm // chunk

    def scores(ci):
        rows = slice(ci * chunk, (ci + 1) * chunk)
        q = p_scr[rows, 0:hk] * q_scale
        k = p_scr[rows, hk:2 * hk]
        v = p_scr[rows, 2 * hk:2 * hk + hv].astype(_BF16)
        return _gla_chunk_scores(q, k, v, g_scr[rows, :], chunk)

    pre = scores(0)
    for ci in range(n_chunks):
        nxt = scores(ci + 1) if ci + 1 < n_chunks else None
        o_scr[ci * chunk:(ci + 1) * chunk, :] = _gla_chunk_apply(pre, s_scr, hg_ref[...])
        pre = nxt

    r = p_scr[:, 2 * hk + hv:2 * hk + 2 * hv]
    o_ref[0] = x + _dot((o_scr[...] * _silu(r)).astype(_BF16), wout_ref[...])

    @pl.when(t == n_t - 1)
    def _():
        sout_ref[0] = s_scr[...]


def _gla_call(x, s0, w, *, n_seq, tm, chunk):
    _, seq, d = x.shape
    n_h, dk, dv = s0.shape[1:]
    hk, hv = n_h * dk, n_h * dv
    shared = s0.shape[0] == 1 and n_seq > 1
    out, s_out = pl.pallas_call(
        functools.partial(_gla_kernel, chunk=chunk),
        grid=(n_seq, seq // tm),
        in_specs=[
            pl.BlockSpec((1, tm, d), lambda b, t: (b, t, 0)),
            _init_spec(s0.shape, shared),
            _const_spec(w["gn"].shape), _const_spec(w["win"].shape),
            _const_spec(w["wgd"].shape), _const_spec(w["wgu"].shape),
            _const_spec(w["bgate"].shape), _const_spec(w["hg"].shape),
            _const_spec(w["wout"].shape),
        ],
        out_specs=[
            pl.BlockSpec((1, tm, d), lambda b, t: (b, t, 0)),
            pl.BlockSpec((1, n_h, dk, dv), lambda b, t: (b, 0, 0, 0)),
        ],
        out_shape=[
            jax.ShapeDtypeStruct((n_seq, seq, d), _F32),
            jax.ShapeDtypeStruct((n_seq, n_h, dk, dv), _F32),
        ],
        scratch_shapes=[
            pltpu.VMEM((n_h, dk, dv), _F32),
            pltpu.VMEM((tm, 2 * hk + 2 * hv), _F32),
            pltpu.VMEM((tm, hk), _F32),
            pltpu.VMEM((tm, hv), _F32),
        ],
        compiler_params=_compiler_params(),
        name="gla_mixer",
    )(x, s0, w["gn"], w["win"], w["wgd"], w["wgu"], w["bgate"], w["hg"], w["wout"])
    return out, s_out


def _row(v):
    return v.reshape(1, -1).astype(_F32)


def _prep_gla(norm, w_in, w_gate_up, b_gate, head_gain, w_out):
    hk = w_gate_up.shape[1]
    main = w_in.shape[1] - GLA_GATE_RANK
    wgd = jnp.pad(w_in[:, main:], ((0, 0), (0, LANES - GLA_GATE_RANK)))
    wgu = jnp.pad(w_gate_up, ((0, LANES - GLA_GATE_RANK), (0, 0)))
    return dict(gn=_row(norm), win=w_in[:, :main].astype(_BF16), wgd=wgd.astype(_BF16),
                wgu=wgu.astype(_BF16), bgate=_row(b_gate), hg=_row(head_gain),
                wout=w_out.astype(_BF16))


def _prep_sc(norm, w_in, conv_w, w_out):
    d = w_in.shape[0]
    win = w_in.reshape(d, 3, d).transpose(1, 0, 2).astype(_BF16)
    return dict(gn=_row(norm), win=win, cw=conv_w.astype(_F32), wout=w_out.astype(_BF16))


def _prep_ffn(norm, w_up, conv_w, conv_b, w_down):
    d, f2 = w_up.shape
    f = f2 // 2
    fc = FFN_COLS
    n_c = f // fc

    def cols(a):
        return a.reshape(a.shape[0], n_c, fc).transpose(1, 0, 2)

    return dict(gn=_row(norm),
                wg=cols(w_up[:, :f]).astype(_BF16), wu=cols(w_up[:, f:]).astype(_BF16),
                cwg=cols(conv_w[:, :f]).astype(_F32), cwu=cols(conv_w[:, f:]).astype(_F32),
                cbg=cols(conv_b[None, :f]).astype(_F32), cbu=cols(conv_b[None, f:]).astype(_F32),
                wd=w_down.reshape(n_c, fc, d).astype(_BF16))


def _tail_from_cache(cache):
    return jnp.pad(cache, ((0, 0), (SUBLANES - (CONV_W - 1), 0), (0, 0)))


def _ffn_tail_from_cache(cache, n_c, fc):
    b = cache.shape[0]
    t = _tail_from_cache(cache).reshape(b, SUBLANES, 2, n_c, fc)
    return t.transpose(0, 2, 3, 1, 4)


def _ffn_cache_from_tail(tail):
    b = tail.shape[0]
    rows = tail[:, :, :, SUBLANES - (CONV_W - 1):, :]
    return rows.transpose(0, 3, 1, 2, 4).reshape(b, CONV_W - 1, -1)


def _trunk(x, gla_s, conv_tail, ffn_tail, params, gfin, *, tm_gla, tm_sc, tm_ffn, chunk):
    n_seq = x.shape[0]
    depth = len(params)
    gla_new, conv_new, ffn_new = [], [], []
    i_gla = i_conv = 0
    for i, (mix, ffn) in enumerate(params):
        if i % 2 == 0:
            x, s = _gla_call(x, gla_s[i_gla], mix, n_seq=n_seq, tm=tm_gla, chunk=chunk)
            gla_new.append(s)
            i_gla += 1
        else:
            x, s = _sc_call(x, conv_tail[i_conv], mix, n_seq=n_seq, tm=tm_sc)
            conv_new.append(s)
            i_conv += 1
        x, s = _ffn_call(x, ffn_tail[i], ffn, gfin, n_seq=n_seq, tm=tm_ffn,
                         final_norm=(i == depth - 1))
        ffn_new.append(s)
    return x, gla_new, conv_new, ffn_new


def kernel(x_prompt, x_sample, state_gla, cache_conv, cache_ffn, meta, norm_mix, norm_ffn, norm_final,
           gla_w_in, gla_w_gate_up, gla_b_gate, gla_head_gain, gla_w_out,
           sc_w_in, sc_conv_w, sc_w_out, ffn_w_up, ffn_conv_w, ffn_conv_b, ffn_w_down):
    depth = norm_mix.shape[0]
    n_b, seq, d = x_prompt.shape
    n_s, s_len, _ = x_sample.shape
    assert s_len == N_META and seq % GLA_CHUNK == 0
    f = ffn_w_down.shape[1]
    n_c, fc = f // FFN_COLS, FFN_COLS

    params = []
    for i in range(depth):
        j = i // 2
        if i % 2 == 0:
            mix = _prep_gla(norm_mix[i], gla_w_in[j], gla_w_gate_up[j], gla_b_gate[j],
                            gla_head_gain[j], gla_w_out[j])
        else:
            mix = _prep_sc(norm_mix[i], sc_w_in[j], sc_conv_w[j], sc_w_out[j])
        params.append((mix, _prep_ffn(norm_ffn[i], ffn_w_up[i], ffn_conv_w[i], ffn_conv_b[i],
                                      ffn_w_down[i])))
    gfin = _row(norm_final)

    xs = jnp.concatenate([x_sample, meta[None].astype(x_sample.dtype)], axis=0)
    zero1 = lambda a: jnp.zeros((a.shape[0], 1) + a.shape[2:], a.dtype)
    gla_s = jnp.concatenate([state_gla, zero1(state_gla)], axis=1).astype(_F32)
    conv_c = jnp.concatenate([cache_conv, zero1(cache_conv)], axis=1).astype(_F32)
    ffn_c = jnp.concatenate([cache_ffn, zero1(cache_ffn)], axis=1).astype(_F32)
    ys, gla_1, conv_1, ffn_1 = _trunk(
        xs, [gla_s[j] for j in range(gla_s.shape[0])],
        [_tail_from_cache(conv_c[j]) for j in range(conv_c.shape[0])],
        [_ffn_tail_from_cache(ffn_c[i], n_c, fc) for i in range(depth)],
        params, gfin, tm_gla=s_len, tm_sc=s_len, tm_ffn=s_len, chunk=s_len)

    yp, gla_2, conv_2, ffn_2 = _trunk(
        x_prompt, [s[n_s:] for s in gla_1], [s[n_s:] for s in conv_1], [s[n_s:] for s in ffn_1],
        params, gfin, tm_gla=TILE_GLA, tm_sc=TILE_SC, tm_ffn=TILE_FFN, chunk=GLA_CHUNK)

    dt = x_prompt.dtype
    tail2 = lambda s: s[:, SUBLANES - (CONV_W - 1):, :]
    return (yp.astype(dt), ys[:n_s].astype(dt),
            jnp.stack(gla_2).astype(dt), jnp.stack([s[:n_s] for s in gla_1]).astype(dt),
            jnp.stack([tail2(s) for s in conv_2]).astype(dt),
            jnp.stack([tail2(s[:n_s]) for s in conv_1]).astype(dt),
            jnp.stack([_ffn_cache_from_tail(s) for s in ffn_2]).astype(dt),
            jnp.stack([_ffn_cache_from_tail(s[:n_s]) for s in ffn_1]).astype(dt))
```

```python
import functools

import jax
import jax.numpy as jnp
from jax import lax
from jax.experimental import pallas as pl
from jax.experimental.pallas import tpu as pltpu

N_META = 16
GLA_HEADS = 4
GLA_GATE_RANK = 16
GLA_TAU = 16.0
CONV_W = 3
EPS = 1e-6
GLA_CHUNK = 64
FFN_COLS = 256
TILE_GLA = 256
TILE_SC = 256
TILE_FFN = 512
LANES = 128
SUBLANES = 8
VMEM_LIMIT_BYTES = 56 * 1024 * 1024

_BF16 = jnp.bfloat16
_F32 = jnp.float32
_LOG2E = 1.4426950408889634


def _dot(a, b):
    return jnp.dot(a, b, preferred_element_type=_F32)


def _dot_nt(a, b):
    return lax.dot_general(a, b, (((1,), (1,)), ((), ())), preferred_element_type=_F32)


def _dot_tn(a, b):
    return lax.dot_general(a, b, (((0,), (0,)), ((), ())), preferred_element_type=_F32)


def _rmsnorm(x, g):
    return x * lax.rsqrt(jnp.mean(x * x, axis=-1, keepdims=True) + EPS) * g


def _silu(x):
    return x / (1.0 + jnp.exp2(x * (-_LOG2E)))


def _log_sigmoid(z):
    return jnp.minimum(z, 0.0) - jnp.log(1.0 + jnp.exp(-jnp.abs(z)))


def _shift_rows(z, tail, k):
    rolled = pltpu.roll(z, k, 0)
    head = rolled[:SUBLANES]
    rid = lax.broadcasted_iota(jnp.int32, head.shape, 0)
    head = jnp.where(rid < k, pltpu.roll(tail, k, 0), head)
    if z.shape[0] == SUBLANES:
        return head
    return jnp.concatenate([head, rolled[SUBLANES:]], axis=0)


def _causal_conv(z, tail, w):
    return w[2:3] * z + w[1:2] * _shift_rows(z, tail, 1) + w[0:1] * _shift_rows(z, tail, 2)


def _const_spec(shape):
    nd = len(shape)
    return pl.BlockSpec(shape, lambda b, t: (0,) * nd, pipeline_mode=pl.Buffered(1))


def _init_spec(shape, shared):
    nd = len(shape)
    block = (1,) + tuple(shape[1:])
    if shared:
        return pl.BlockSpec(block, lambda b, t: (0,) * nd)
    return pl.BlockSpec(block, lambda b, t: (b,) + (0,) * (nd - 1))


def _tile_spec(tm, d):
    return pl.BlockSpec((1, tm, d), lambda b, t: (b, t, 0))


def _next_tile_spec(tm, d, n_t):
    return pl.BlockSpec((1, tm, d), lambda b, t: (b, jnp.minimum(t + 1, n_t - 1), 0))


def _compiler_params():
    return pltpu.CompilerParams(dimension_semantics=("arbitrary", "arbitrary"),
                                vmem_limit_bytes=VMEM_LIMIT_BYTES)


def _ffn_kernel(x_ref, init_ref, gn_ref, wg_ref, wu_ref, cwg_ref, cwu_ref, cbg_ref, cbu_ref, wd_ref,
                gfin_ref, o_ref, cache_ref, zg_buf, zu_buf, *, final_norm):
    t = pl.program_id(1)
    n_t = pl.num_programs(1)
    n_c = wg_ref.shape[0]
    tm = x_ref.shape[1]
    s8 = SUBLANES

    @pl.when(t == 0)
    def _():
        zg_buf[:, 0:s8, :] = init_ref[0, 0]
        zu_buf[:, 0:s8, :] = init_ref[0, 1]

    x = x_ref[0]
    h = _rmsnorm(x, gn_ref[...]).astype(_BF16)

    def up(j):
        zg_buf[j, s8:s8 + tm, :] = _dot(h, wg_ref[j])
        zu_buf[j, s8:s8 + tm, :] = _dot(h, wu_ref[j])

    def conv(buf, cw_ref, cb_ref, j):
        w = cw_ref[j]
        return (w[2:3] * buf[j, s8:s8 + tm, :] + w[1:2] * buf[j, s8 - 1:s8 - 1 + tm, :]
                + w[0:1] * buf[j, s8 - 2:s8 - 2 + tm, :] + cb_ref[j])

    def act(j):
        a = _silu(conv(zg_buf, cwg_ref, cbg_ref, j)) * conv(zu_buf, cwu_ref, cbu_ref, j)
        zg_buf[j, 0:s8, :] = zg_buf[j, tm:tm + s8, :]
        zu_buf[j, 0:s8, :] = zu_buf[j, tm:tm + s8, :]
        return a.astype(_BF16)

    up(0)
    if n_c > 1:
        up(1)
    a_prev = act(0)
    acc = x
    for j in range(1, n_c):
        a_cur = act(j)
        if j + 1 < n_c:
            up(j + 1)
        acc = acc + _dot(a_prev, wd_ref[j - 1])
        a_prev = a_cur
    acc = acc + _dot(a_prev, wd_ref[n_c - 1])
    if final_norm:
        acc = _rmsnorm(acc, gfin_ref[...])
    o_ref[0] = acc

    @pl.when(t == n_t - 1)
    def _():
        cache_ref[0, 0] = zg_buf[:, 0:s8, :]
        cache_ref[0, 1] = zu_buf[:, 0:s8, :]


def _ffn_call(x, init, w, gfin, *, n_seq, tm, final_norm):
    _, seq, d = x.shape
    n_c = w["wg"].shape[0]
    fc = w["wg"].shape[2]
    shared = init.shape[0] == 1 and n_seq > 1
    tail_shape = (n_c, SUBLANES, fc)
    out, cache = pl.pallas_call(
        functools.partial(_ffn_kernel, final_norm=final_norm),
        grid=(n_seq, seq // tm),
        in_specs=[
            _tile_spec(tm, d),
            _init_spec(init.shape, shared),
            _const_spec(w["gn"].shape),
            _const_spec(w["wg"].shape), _const_spec(w["wu"].shape),
            _const_spec(w["cwg"].shape), _const_spec(w["cwu"].shape),
            _const_spec(w["cbg"].shape), _const_spec(w["cbu"].shape),
            _const_spec(w["wd"].shape),
            _const_spec(gfin.shape),
        ],
        out_specs=[
            _tile_spec(tm, d),
            pl.BlockSpec((1, 2) + tail_shape, lambda b, t: (b, 0, 0, 0, 0)),
        ],
        out_shape=[
            jax.ShapeDtypeStruct((n_seq, seq, d), _F32),
            jax.ShapeDtypeStruct((n_seq, 2) + tail_shape, _F32),
        ],
        scratch_shapes=[pltpu.VMEM((n_c, SUBLANES + tm, fc), _F32),
                        pltpu.VMEM((n_c, SUBLANES + tm, fc), _F32)],
        compiler_params=_compiler_params(),
        name="conv_ffn",
    )(x, init, w["gn"], w["wg"], w["wu"], w["cwg"], w["cwu"], w["cbg"], w["cbu"], w["wd"], gfin)
    return out, cache


def _sc_kernel(x_ref, xn_ref, init_ref, gn_ref, win_ref, cw_ref, wout_ref, o_ref, cache_ref,
               tail, p_scr, *, pipelined):
    t = pl.program_id(1)
    n_t = pl.num_programs(1)
    tm = x_ref.shape[1]

    def project(h, slot):
        for i in range(3):
            p_scr[slot, i] = _dot(h, win_ref[i])

    @pl.when(t == 0)
    def _():
        tail[...] = init_ref[0]
        project(_rmsnorm(x_ref[0], gn_ref[...]).astype(_BF16), 0)

    if pipelined:
        cur = lax.rem(t, 2)
        project(_rmsnorm(xn_ref[0], gn_ref[...]).astype(_BF16), 1 - cur)
    else:
        cur = 0
    ci = p_scr[cur, 1] * p_scr[cur, 2]
    y = _causal_conv(ci, tail[...], cw_ref[...])
    tail[...] = ci[tm - SUBLANES:]
    o_ref[0] = x_ref[0] + _dot((p_scr[cur, 0] * y).astype(_BF16), wout_ref[...])

    @pl.when(t == n_t - 1)
    def _():
        cache_ref[0] = tail[...]


def _sc_call(x, init, w, *, n_seq, tm):
    _, seq, d = x.shape
    n_t = seq // tm
    shared = init.shape[0] == 1 and n_seq > 1
    out, cache = pl.pallas_call(
        functools.partial(_sc_kernel, pipelined=n_t > 1),
        grid=(n_seq, n_t),
        in_specs=[
            _tile_spec(tm, d), _next_tile_spec(tm, d, n_t),
            _init_spec(init.shape, shared),
            _const_spec(w["gn"].shape), _const_spec(w["win"].shape),
            _const_spec(w["cw"].shape), _const_spec(w["wout"].shape),
        ],
        out_specs=[
            _tile_spec(tm, d),
            pl.BlockSpec((1, SUBLANES, d), lambda b, t: (b, 0, 0)),
        ],
        out_shape=[
            jax.ShapeDtypeStruct((n_seq, seq, d), _F32),
            jax.ShapeDtypeStruct((n_seq, SUBLANES, d), _F32),
        ],
        scratch_shapes=[pltpu.VMEM((SUBLANES, d), _F32), pltpu.VMEM((2, 3, tm, d), _F32)],
        compiler_params=_compiler_params(),
        name="shortconv_mixer",
    )(x, x, init, w["gn"], w["win"], w["cw"], w["wout"])
    return out, cache


def _cumsum_rows(a):
    n = a.shape[0]
    rid = lax.broadcasted_iota(jnp.int32, (SUBLANES, a.shape[1]), 0)
    groups = []
    for i in range(0, n, SUBLANES):
        grp = a[i:i + SUBLANES]
        s = 1
        while s < SUBLANES:
            grp = grp + jnp.where(rid >= s, pltpu.roll(grp, s, 0), 0.0)
            s *= 2
        if groups:
            grp = grp + groups[-1][SUBLANES - 1:SUBLANES]
        groups.append(grp)
    return jnp.concatenate(groups, axis=0) if len(groups) > 1 else groups[0]


def _level_masks(c):
    tt = lax.broadcasted_iota(jnp.int32, (c, c), 0)
    ss = lax.broadcasted_iota(jnp.int32, (c, c), 1)
    txs = tt ^ ss
    lower = tt > ss
    masks = [tt == ss]
    half = 1
    while half < c:
        masks.append(lower & (txs >= half) & (txs < 2 * half))
        half *= 2
    return masks


def _gla_chunk_scores(q, k, v, g, c, masks):
    n_h = GLA_HEADS
    dk = q.shape[1] // n_h
    b = _cumsum_rows(g)
    b_last = b[c - 1:c]
    qe = (q * jnp.exp(b)).astype(_BF16)
    kd = (k * jnp.exp(b_last - b)).astype(_BF16)
    decay = jnp.exp(b_last)

    rid = lax.broadcasted_iota(jnp.int32, b.shape, 0)
    levels = [(q.astype(_BF16), k.astype(_BF16))]
    e_k = b
    half = 1
    while half < c:
        hi = (rid & half) != 0
        b_mid = jnp.where(hi, pltpu.roll(e_k, half, 0), e_k)
        scale = jnp.exp2((b - b_mid) * jnp.where(hi, _LOG2E, -_LOG2E))
        x = (jnp.where(hi, q, k) * scale).astype(_BF16)
        levels.append((x, x))
        if 2 * half < c:
            e_k = jnp.where(hi, e_k, pltpu.roll(e_k, c - half, 0))
        half *= 2

    scores = []
    for hh in range(n_h):
        ks = slice(hh * dk, (hh + 1) * dk)
        a = jnp.zeros((c, c), _F32)
        for mask, (qs, kk) in zip(masks, levels):
            a = jnp.where(mask, _dot_nt(qs[:, ks], kk[:, ks]), a)
        scores.append(a.astype(_BF16))
    return dict(scores=scores, qe=qe, kd=kd, decay=decay, v=v)


def _gla_chunk_apply(pre, s_ref, hg):
    n_h = GLA_HEADS
    qe, kd, decay, v = pre["qe"], pre["kd"], pre["decay"], pre["v"]
    dk = qe.shape[1] // n_h
    dv = v.shape[1] // n_h
    outs = []
    for hh in range(n_h):
        ks = slice(hh * dk, (hh + 1) * dk)
        vs = slice(hh * dv, (hh + 1) * dv)
        s_old = s_ref[hh]
        o = _dot(qe[:, ks], s_old.astype(_BF16)) + _dot(pre["scores"][hh], v[:, vs])
        o = o * lax.rsqrt(jnp.mean(o * o, axis=-1, keepdims=True) + EPS) * hg[:, vs]
        outs.append(o)
        dcol = jnp.transpose(jnp.broadcast_to(decay[:, ks], (dk, dk)))
        dmat = jnp.concatenate([dcol] * (dv // dk), axis=1)
        s_ref[hh] = dmat * s_old + _dot_tn(kd[:, ks], v[:, vs])
    return jnp.concatenate(outs, axis=1)


def _gla_kernel(x_ref, xn_ref, s0_ref, gn_ref, win_ref, wgd_ref, wgu_ref, bgate_ref, hg_ref, wout_ref,
                o_ref, sout_ref, s_scr, p_scr, g_scr, o_scr, *, chunk, pipelined):
    t = pl.program_id(1)
    n_t = pl.num_programs(1)
    tm = x_ref.shape[1]
    hk = g_scr.shape[2]
    hv = o_scr.shape[1]
    n_col = win_ref.shape[1]
    n_chunks = tm // chunk

    def project_gate(h, slot):
        gdown = _dot(h, wgd_ref[...]).astype(_BF16)
        z = _dot(gdown, wgu_ref[...]) + bgate_ref[...]
        g_scr[slot] = _log_sigmoid(z) * (1.0 / GLA_TAU)

    def project_piece(h, slot, i, n):
        c0, c1 = (i * n_col) // n, ((i + 1) * n_col) // n
        p_scr[slot, :, c0:c1] = _dot(h, win_ref[:, c0:c1])

    @pl.when(t == 0)
    def _():
        s_scr[...] = s0_ref[0]
        h0 = _rmsnorm(x_ref[0], gn_ref[...]).astype(_BF16)
        project_gate(h0, 0)
        project_piece(h0, 0, 0, 1)

    if pipelined:
        cur = lax.rem(t, 2)
        nxt = 1 - cur
        h_next = _rmsnorm(xn_ref[0], gn_ref[...]).astype(_BF16)
    else:
        cur = 0
    q_scale = float(hk // GLA_HEADS) ** -0.5
    masks = _level_masks(chunk)

    def scores(ci):
        rows = slice(ci * chunk, (ci + 1) * chunk)
        q = p_scr[cur, rows, 0:hk] * q_scale
        k = p_scr[cur, rows, hk:2 * hk]
        v = p_scr[cur, rows, 2 * hk:2 * hk + hv].astype(_BF16)
        return _gla_chunk_scores(q, k, v, g_scr[cur, rows, :], chunk, masks)

    pre = scores(0)
    if pipelined:
        project_gate(h_next, nxt)
    for ci in range(n_chunks):
        if pipelined:
            project_piece(h_next, nxt, ci, n_chunks)
        nxt_pre = scores(ci + 1) if ci + 1 < n_chunks else None
        o_scr[ci * chunk:(ci + 1) * chunk, :] = _gla_chunk_apply(pre, s_scr, hg_ref[...])
        pre = nxt_pre

    r = p_scr[cur, :, 2 * hk + hv:2 * hk + 2 * hv]
    o_ref[0] = x_ref[0] + _dot((o_scr[...] * _silu(r)).astype(_BF16), wout_ref[...])

    @pl.when(t == n_t - 1)
    def _():
        sout_ref[0] = s_scr[...]


def _gla_call(x, s0, w, *, n_seq, tm, chunk):
    _, seq, d = x.shape
    n_h, dk, dv = s0.shape[1:]
    hk, hv = n_h * dk, n_h * dv
    n_t = seq // tm
    shared = s0.shape[0] == 1 and n_seq > 1
    out, s_out = pl.pallas_call(
        functools.partial(_gla_kernel, chunk=chunk, pipelined=n_t > 1),
        grid=(n_seq, n_t),
        in_specs=[
            _tile_spec(tm, d), _next_tile_spec(tm, d, n_t),
            _init_spec(s0.shape, shared),
            _const_spec(w["gn"].shape), _const_spec(w["win"].shape),
            _const_spec(w["wgd"].shape), _const_spec(w["wgu"].shape),
            _const_spec(w["bgate"].shape), _const_spec(w["hg"].shape),
            _const_spec(w["wout"].shape),
        ],
        out_specs=[
            _tile_spec(tm, d),
            pl.BlockSpec((1, n_h, dk, dv), lambda b, t: (b, 0, 0, 0)),
        ],
        out_shape=[
            jax.ShapeDtypeStruct((n_seq, seq, d), _F32),
            jax.ShapeDtypeStruct((n_seq, n_h, dk, dv), _F32),
        ],
        scratch_shapes=[
            pltpu.VMEM((n_h, dk, dv), _F32),
            pltpu.VMEM((2, tm, 2 * hk + 2 * hv), _F32),
            pltpu.VMEM((2, tm, hk), _F32),
            pltpu.VMEM((tm, hv), _F32),
        ],
        compiler_params=_compiler_params(),
        name="gla_mixer",
    )(x, x, s0, w["gn"], w["win"], w["wgd"], w["wgu"], w["bgate"], w["hg"], w["wout"])
    return out, s_out


def _row(v):
    return v.reshape(1, -1).astype(_F32)


def _prep_gla(norm, w_in, w_gate_up, b_gate, head_gain, w_out):
    main = w_in.shape[1] - GLA_GATE_RANK
    wgd = jnp.pad(w_in[:, main:], ((0, 0), (0, LANES - GLA_GATE_RANK)))
    wgu = jnp.pad(w_gate_up, ((0, LANES - GLA_GATE_RANK), (0, 0)))
    return dict(gn=_row(norm), win=w_in[:, :main].astype(_BF16), wgd=wgd.astype(_BF16),
                wgu=wgu.astype(_BF16), bgate=_row(b_gate), hg=_row(head_gain),
                wout=w_out.astype(_BF16))


def _prep_sc(norm, w_in, conv_w, w_out):
    d = w_in.shape[0]
    win = w_in.reshape(d, 3, d).transpose(1, 0, 2).astype(_BF16)
    return dict(gn=_row(norm), win=win, cw=conv_w.astype(_F32), wout=w_out.astype(_BF16))


def _prep_ffn(norm, w_up, conv_w, conv_b, w_down):
    d, f2 = w_up.shape
    f = f2 // 2
    fc = FFN_COLS
    n_c = f // fc

    def cols(a):
        return a.reshape(a.shape[0], n_c, fc).transpose(1, 0, 2)

    return dict(gn=_row(norm),
                wg=cols(w_up[:, :f]).astype(_BF16), wu=cols(w_up[:, f:]).astype(_BF16),
                cwg=cols(conv_w[:, :f]).astype(_F32), cwu=cols(conv_w[:, f:]).astype(_F32),
                cbg=cols(conv_b[None, :f]).astype(_F32), cbu=cols(conv_b[None, f:]).astype(_F32),
                wd=w_down.reshape(n_c, fc, d).astype(_BF16))


def _tail_from_cache(cache):
    return jnp.pad(cache, ((0, 0), (SUBLANES - (CONV_W - 1), 0), (0, 0)))


def _ffn_tail_from_cache(cache, n_c, fc):
    b = cache.shape[0]
    t = _tail_from_cache(cache).reshape(b, SUBLANES, 2, n_c, fc)
    return t.transpose(0, 2, 3, 1, 4)


def _ffn_cache_from_tail(tail):
    b = tail.shape[0]
    rows = tail[:, :, :, SUBLANES - (CONV_W - 1):, :]
    return rows.transpose(0, 3, 1, 2, 4).reshape(b, CONV_W - 1, -1)


def _trunk(x, gla_s, conv_tail, ffn_tail, params, gfin, *, tm_gla, tm_sc, tm_ffn, chunk):
    n_seq = x.shape[0]
    depth = len(params)
    gla_new, conv_new, ffn_new = [], [], []
    i_gla = i_conv = 0
    for i, (mix, ffn) in enumerate(params):
        if i % 2 == 0:
            x, s = _gla_call(x, gla_s[i_gla], mix, n_seq=n_seq, tm=tm_gla, chunk=chunk)
            gla_new.append(s)
            i_gla += 1
        else:
            x, s = _sc_call(x, conv_tail[i_conv], mix, n_seq=n_seq, tm=tm_sc)
            conv_new.append(s)
            i_conv += 1
        x, s = _ffn_call(x, ffn_tail[i], ffn, gfin, n_seq=n_seq, tm=tm_ffn,
                         final_norm=(i == depth - 1))
        ffn_new.append(s)
    return x, gla_new, conv_new, ffn_new


def kernel(x_prompt, x_sample, state_gla, cache_conv, cache_ffn, meta, norm_mix, norm_ffn, norm_final,
           gla_w_in, gla_w_gate_up, gla_b_gate, gla_head_gain, gla_w_out,
           sc_w_in, sc_conv_w, sc_w_out, ffn_w_up, ffn_conv_w, ffn_conv_b, ffn_w_down):
    depth = norm_mix.shape[0]
    n_b, seq, d = x_prompt.shape
    n_s, s_len, _ = x_sample.shape
    assert s_len == N_META and seq % GLA_CHUNK == 0
    f = ffn_w_down.shape[1]
    n_c, fc = f // FFN_COLS, FFN_COLS

    params = []
    for i in range(depth):
        j = i // 2
        if i % 2 == 0:
            mix = _prep_gla(norm_mix[i], gla_w_in[j], gla_w_gate_up[j], gla_b_gate[j],
                            gla_head_gain[j], gla_w_out[j])
        else:
            mix = _prep_sc(norm_mix[i], sc_w_in[j], sc_conv_w[j], sc_w_out[j])
        params.append((mix, _prep_ffn(norm_ffn[i], ffn_w_up[i], ffn_conv_w[i], ffn_conv_b[i],
                                      ffn_w_down[i])))
    gfin = _row(norm_final)

    xs = jnp.concatenate([x_sample, meta[None].astype(x_sample.dtype)], axis=0)
    zero1 = lambda a: jnp.zeros((a.shape[0], 1) + a.shape[2:], a.dtype)
    gla_s = jnp.concatenate([state_gla, zero1(state_gla)], axis=1).astype(_F32)
    conv_c = jnp.concatenate([cache_conv, zero1(cache_conv)], axis=1).astype(_F32)
    ffn_c = jnp.concatenate([cache_ffn, zero1(cache_ffn)], axis=1).astype(_F32)
    ys, gla_1, conv_1, ffn_1 = _trunk(
        xs, [gla_s[j] for j in range(gla_s.shape[0])],
        [_tail_from_cache(conv_c[j]) for j in range(conv_c.shape[0])],
        [_ffn_tail_from_cache(ffn_c[i], n_c, fc) for i in range(depth)],
        params, gfin, tm_gla=s_len, tm_sc=s_len, tm_ffn=s_len, chunk=s_len)

    yp, gla_2, conv_2, ffn_2 = _trunk(
        x_prompt, [s[n_s:] for s in gla_1], [s[n_s:] for s in conv_1], [s[n_s:] for s in ffn_1],
        params, gfin, tm_gla=min(TILE_GLA, seq), tm_sc=min(TILE_SC, seq), tm_ffn=min(TILE_FFN, seq),
        chunk=GLA_CHUNK)

    dt = x_prompt.dtype
    tail2 = lambda s: s[:, SUBLANES - (CONV_W - 1):, :]
    return (yp.astype(dt), ys[:n_s].astype(dt),
            jnp.stack(gla_2).astype(dt), jnp.stack([s[:n_s] for s in gla_1]).astype(dt),
            jnp.stack([tail2(s) for s in conv_2]).astype(dt),
            jnp.stack([tail2(s[:n_s]) for s in conv_1]).astype(dt),
            jnp.stack([_ffn_cache_from_tail(s) for s in ffn_2]).astype(dt),
            jnp.stack([_ffn_cache_from_tail(s[:n_s]) for s in ffn_1]).astype(dt))
```

```python
import functools

import jax
import jax.numpy as jnp
from jax import lax
from jax.experimental import pallas as pl
from jax.experimental.pallas import tpu as pltpu

N_META = 16
GLA_HEADS = 4
GLA_GATE_RANK = 16
GLA_TAU = 16.0
CONV_W = 3
EPS = 1e-6
GLA_CHUNK = 64
MILD_LOG_DECAY = -60.0
FFN_COLS = 256
TILE_GLA = 256
TILE_SC = 512
TILE_FFN = 512
LANES = 128
SUBLANES = 8
VMEM_LIMIT_BYTES = 56 * 1024 * 1024

_BF16 = jnp.bfloat16
_F32 = jnp.float32
_LOG2E = 1.4426950408889634


def _dot(a, b):
    return jnp.dot(a, b, preferred_element_type=_F32)


def _dot_nt(a, b):
    return lax.dot_general(a, b, (((1,), (1,)), ((), ())), preferred_element_type=_F32)


def _dot_tn(a, b):
    return lax.dot_general(a, b, (((0,), (0,)), ((), ())), preferred_element_type=_F32)


def _rmsnorm(x, g):
    return x * lax.rsqrt(jnp.mean(x * x, axis=-1, keepdims=True) + EPS) * g


def _silu(x):
    return x / (1.0 + jnp.exp2(x * (-_LOG2E)))


def _log_sigmoid(z):
    return jnp.minimum(z, 0.0) - jnp.log(1.0 + jnp.exp(-jnp.abs(z)))


def _shift_rows(z, tail, k):
    rolled = pltpu.roll(z, k, 0)
    head = rolled[:SUBLANES]
    rid = lax.broadcasted_iota(jnp.int32, head.shape, 0)
    head = jnp.where(rid < k, pltpu.roll(tail, k, 0), head)
    if z.shape[0] == SUBLANES:
        return head
    return jnp.concatenate([head, rolled[SUBLANES:]], axis=0)


def _causal_conv(z, tail, w):
    return w[2:3] * z + w[1:2] * _shift_rows(z, tail, 1) + w[0:1] * _shift_rows(z, tail, 2)


def _const_spec(shape):
    nd = len(shape)
    return pl.BlockSpec(shape, lambda b, t: (0,) * nd, pipeline_mode=pl.Buffered(1))


def _init_spec(shape, shared):
    nd = len(shape)
    block = (1,) + tuple(shape[1:])
    if shared:
        return pl.BlockSpec(block, lambda b, t: (0,) * nd)
    return pl.BlockSpec(block, lambda b, t: (b,) + (0,) * (nd - 1))


def _tile_spec(tm, d):
    return pl.BlockSpec((1, tm, d), lambda b, t: (b, t, 0))


def _next_tile_spec(tm, d, n_t):
    return pl.BlockSpec((1, tm, d), lambda b, t: (b, jnp.minimum(t + 1, n_t - 1), 0))


def _compiler_params():
    return pltpu.CompilerParams(dimension_semantics=("arbitrary", "arbitrary"),
                                vmem_limit_bytes=VMEM_LIMIT_BYTES)


def _ffn_kernel(x_ref, init_ref, gn_ref, wg_ref, wu_ref, cwg_ref, cwu_ref, cbg_ref, cbu_ref, wd_ref,
                gfin_ref, o_ref, cache_ref, zg_buf, zu_buf, *, final_norm):
    t = pl.program_id(1)
    n_t = pl.num_programs(1)
    n_c = wg_ref.shape[0]
    tm = x_ref.shape[1]
    s8 = SUBLANES

    @pl.when(t == 0)
    def _():
        zg_buf[:, 0:s8, :] = init_ref[0, 0]
        zu_buf[:, 0:s8, :] = init_ref[0, 1]

    x = x_ref[0]
    h = _rmsnorm(x, gn_ref[...]).astype(_BF16)

    def up(j):
        zg_buf[j, s8:s8 + tm, :] = _dot(h, wg_ref[j])
        zu_buf[j, s8:s8 + tm, :] = _dot(h, wu_ref[j])

    def conv(buf, cw_ref, cb_ref, j):
        w = cw_ref[j]
        return (w[2:3] * buf[j, s8:s8 + tm, :] + w[1:2] * buf[j, s8 - 1:s8 - 1 + tm, :]
                + w[0:1] * buf[j, s8 - 2:s8 - 2 + tm, :] + cb_ref[j])

    def act(j):
        a = _silu(conv(zg_buf, cwg_ref, cbg_ref, j)) * conv(zu_buf, cwu_ref, cbu_ref, j)
        zg_buf[j, 0:s8, :] = zg_buf[j, tm:tm + s8, :]
        zu_buf[j, 0:s8, :] = zu_buf[j, tm:tm + s8, :]
        return a.astype(_BF16)

    up(0)
    if n_c > 1:
        up(1)
    a_prev = act(0)
    acc = x
    for j in range(1, n_c):
        a_cur = act(j)
        if j + 1 < n_c:
            up(j + 1)
        acc = acc + _dot(a_prev, wd_ref[j - 1])
        a_prev = a_cur
    acc = acc + _dot(a_prev, wd_ref[n_c - 1])
    if final_norm:
        acc = _rmsnorm(acc, gfin_ref[...])
    o_ref[0] = acc

    @pl.when(t == n_t - 1)
    def _():
        cache_ref[0, 0] = zg_buf[:, 0:s8, :]
        cache_ref[0, 1] = zu_buf[:, 0:s8, :]


def _ffn_call(x, init, w, gfin, *, n_seq, tm, final_norm):
    _, seq, d = x.shape
    n_c = w["wg"].shape[0]
    fc = w["wg"].shape[2]
    shared = init.shape[0] == 1 and n_seq > 1
    tail_shape = (n_c, SUBLANES, fc)
    out, cache = pl.pallas_call(
        functools.partial(_ffn_kernel, final_norm=final_norm),
        grid=(n_seq, seq // tm),
        in_specs=[
            _tile_spec(tm, d),
            _init_spec(init.shape, shared),
            _const_spec(w["gn"].shape),
            _const_spec(w["wg"].shape), _const_spec(w["wu"].shape),
            _const_spec(w["cwg"].shape), _const_spec(w["cwu"].shape),
            _const_spec(w["cbg"].shape), _const_spec(w["cbu"].shape),
            _const_spec(w["wd"].shape),
            _const_spec(gfin.shape),
        ],
        out_specs=[
            _tile_spec(tm, d),
            pl.BlockSpec((1, 2) + tail_shape, lambda b, t: (b, 0, 0, 0, 0)),
        ],
        out_shape=[
            jax.ShapeDtypeStruct((n_seq, seq, d), _F32),
            jax.ShapeDtypeStruct((n_seq, 2) + tail_shape, _F32),
        ],
        scratch_shapes=[pltpu.VMEM((n_c, SUBLANES + tm, fc), _F32),
                        pltpu.VMEM((n_c, SUBLANES + tm, fc), _F32)],
        compiler_params=_compiler_params(),
        name="conv_ffn",
    )(x, init, w["gn"], w["wg"], w["wu"], w["cwg"], w["cwu"], w["cbg"], w["cbu"], w["wd"], gfin)
    return out, cache


def _sc_kernel(x_ref, xn_ref, init_ref, gn_ref, win_ref, cw_ref, wout_ref, o_ref, cache_ref,
               tail, p_scr, *, pipelined):
    t = pl.program_id(1)
    n_t = pl.num_programs(1)
    tm = x_ref.shape[1]

    def project(h, slot):
        for i in range(3):
            p_scr[slot, i] = _dot(h, win_ref[i])

    @pl.when(t == 0)
    def _():
        tail[...] = init_ref[0]
        project(_rmsnorm(x_ref[0], gn_ref[...]).astype(_BF16), 0)

    if pipelined:
        cur = lax.rem(t, 2)
        project(_rmsnorm(xn_ref[0], gn_ref[...]).astype(_BF16), 1 - cur)
    else:
        cur = 0
    ci = p_scr[cur, 1] * p_scr[cur, 2]
    y = _causal_conv(ci, tail[...], cw_ref[...])
    tail[...] = ci[tm - SUBLANES:]
    o_ref[0] = x_ref[0] + _dot((p_scr[cur, 0] * y).astype(_BF16), wout_ref[...])

    @pl.when(t == n_t - 1)
    def _():
        cache_ref[0] = tail[...]


def _sc_call(x, init, w, *, n_seq, tm):
    _, seq, d = x.shape
    n_t = seq // tm
    shared = init.shape[0] == 1 and n_seq > 1
    out, cache = pl.pallas_call(
        functools.partial(_sc_kernel, pipelined=n_t > 1),
        grid=(n_seq, n_t),
        in_specs=[
            _tile_spec(tm, d), _next_tile_spec(tm, d, n_t),
            _init_spec(init.shape, shared),
            _const_spec(w["gn"].shape), _const_spec(w["win"].shape),
            _const_spec(w["cw"].shape), _const_spec(w["wout"].shape),
        ],
        out_specs=[
            _tile_spec(tm, d),
            pl.BlockSpec((1, SUBLANES, d), lambda b, t: (b, 0, 0)),
        ],
        out_shape=[
            jax.ShapeDtypeStruct((n_seq, seq, d), _F32),
            jax.ShapeDtypeStruct((n_seq, SUBLANES, d), _F32),
        ],
        scratch_shapes=[pltpu.VMEM((SUBLANES, d), _F32), pltpu.VMEM((2, 3, tm, d), _F32)],
        compiler_params=_compiler_params(),
        name="shortconv_mixer",
    )(x, x, init, w["gn"], w["win"], w["cw"], w["wout"])
    return out, cache


def _cumsum_rows(a):
    n = a.shape[0]
    rid = lax.broadcasted_iota(jnp.int32, (SUBLANES, a.shape[1]), 0)
    groups = []
    for i in range(0, n, SUBLANES):
        grp = a[i:i + SUBLANES]
        s = 1
        while s < SUBLANES:
            grp = grp + jnp.where(rid >= s, pltpu.roll(grp, s, 0), 0.0)
            s *= 2
        if groups:
            grp = grp + groups[-1][SUBLANES - 1:SUBLANES]
        groups.append(grp)
    return jnp.concatenate(groups, axis=0) if len(groups) > 1 else groups[0]


def _level_masks(c):
    tt = lax.broadcasted_iota(jnp.int32, (c, c), 0)
    ss = lax.broadcasted_iota(jnp.int32, (c, c), 1)
    txs = tt ^ ss
    lower = tt > ss
    masks = [tt == ss]
    half = 1
    while half < c:
        masks.append(lower & (txs >= half) & (txs < 2 * half))
        half *= 2
    return masks


def _gla_chunk_scores(q, k, v, g, c, masks, mild):
    n_h = GLA_HEADS
    dk = q.shape[1] // n_h
    b = _cumsum_rows(g)
    b_last = b[c - 1:c]
    qe = (q * jnp.exp(b)).astype(_BF16)
    kd = (k * jnp.exp(b_last - b)).astype(_BF16)
    decay = jnp.exp(b_last)

    if mild:
        k_inv = (k * jnp.exp(-b)).astype(_BF16)
        tt = lax.broadcasted_iota(jnp.int32, (c, c), 0)
        causal = tt >= lax.broadcasted_iota(jnp.int32, (c, c), 1)
        scores = []
        for hh in range(n_h):
            ks = slice(hh * dk, (hh + 1) * dk)
            a = jnp.where(causal, _dot_nt(qe[:, ks], k_inv[:, ks]), 0.0)
            scores.append(a.astype(_BF16))
        return dict(scores=scores, qe=qe, kd=kd, decay=decay, v=v)

    rid = lax.broadcasted_iota(jnp.int32, b.shape, 0)
    levels = [(q.astype(_BF16), k.astype(_BF16))]
    e_k = b
    half = 1
    while half < c:
        hi = (rid & half) != 0
        b_mid = jnp.where(hi, pltpu.roll(e_k, half, 0), e_k)
        scale = jnp.exp2((b - b_mid) * jnp.where(hi, _LOG2E, -_LOG2E))
        x = (jnp.where(hi, q, k) * scale).astype(_BF16)
        levels.append((x, x))
        if 2 * half < c:
            e_k = jnp.where(hi, e_k, pltpu.roll(e_k, c - half, 0))
        half *= 2

    scores = []
    for hh in range(n_h):
        ks = slice(hh * dk, (hh + 1) * dk)
        a = jnp.zeros((c, c), _F32)
        for mask, (qs, kk) in zip(masks, levels):
            a = jnp.where(mask, _dot_nt(qs[:, ks], kk[:, ks]), a)
        scores.append(a.astype(_BF16))
    return dict(scores=scores, qe=qe, kd=kd, decay=decay, v=v)


def _gla_chunk_apply(pre, s_ref, hg):
    n_h = GLA_HEADS
    qe, kd, decay, v = pre["qe"], pre["kd"], pre["decay"], pre["v"]
    dk = qe.shape[1] // n_h
    dv = v.shape[1] // n_h
    outs = []
    for hh in range(n_h):
        ks = slice(hh * dk, (hh + 1) * dk)
        vs = slice(hh * dv, (hh + 1) * dv)
        s_old = s_ref[hh]
        o = _dot(qe[:, ks], s_old.astype(_BF16)) + _dot(pre["scores"][hh], v[:, vs])
        o = o * lax.rsqrt(jnp.mean(o * o, axis=-1, keepdims=True) + EPS) * hg[:, vs]
        outs.append(o)
        dcol = jnp.transpose(jnp.broadcast_to(decay[:, ks], (dk, dk)))
        dmat = jnp.concatenate([dcol] * (dv // dk), axis=1)
        s_ref[hh] = dmat * s_old + _dot_tn(kd[:, ks], v[:, vs])
    return jnp.concatenate(outs, axis=1)


def _gla_kernel(x_ref, s0_ref, gn_ref, win_ref, wgd_ref, wgu_ref, bgate_ref, hg_ref, wout_ref,
                o_ref, sout_ref, s_scr, p_scr, g_scr, o_scr, *, chunk):
    t = pl.program_id(1)
    n_t = pl.num_programs(1)
    tm = x_ref.shape[1]
    hk = g_scr.shape[1]
    hv = o_scr.shape[1]
    n_chunks = tm // chunk

    @pl.when(t == 0)
    def _():
        s_scr[...] = s0_ref[0]

    x = x_ref[0]
    h = _rmsnorm(x, gn_ref[...]).astype(_BF16)
    p_scr[...] = _dot(h, win_ref[...])
    gdown = _dot(h, wgd_ref[...]).astype(_BF16)
    z = _dot(gdown, wgu_ref[...]) + bgate_ref[...]
    g = _log_sigmoid(z) * (1.0 / GLA_TAU)
    g_scr[...] = g
    chunk_sums = [jnp.sum(g[ci * chunk:(ci + 1) * chunk], axis=0, keepdims=True)
                  for ci in range(n_chunks)]
    lowest = jnp.min(functools.reduce(jnp.minimum, chunk_sums))
    q_scale = float(hk // GLA_HEADS) ** -0.5
    masks = _level_masks(chunk)

    def run_chunks(mild):
        def scores(ci):
            rows = slice(ci * chunk, (ci + 1) * chunk)
            q = p_scr[rows, 0:hk] * q_scale
            k = p_scr[rows, hk:2 * hk]
            v = p_scr[rows, 2 * hk:2 * hk + hv].astype(_BF16)
            return _gla_chunk_scores(q, k, v, g_scr[rows, :], chunk, masks, mild)

        pre = scores(0)
        for ci in range(n_chunks):
            nxt_pre = scores(ci + 1) if ci + 1 < n_chunks else None
            o_scr[ci * chunk:(ci + 1) * chunk, :] = _gla_chunk_apply(pre, s_scr, hg_ref[...])
            pre = nxt_pre

    is_mild = lowest >= MILD_LOG_DECAY

    @pl.when(is_mild)
    def _():
        run_chunks(True)

    @pl.when(jnp.logical_not(is_mild))
    def _():
        run_chunks(False)

    r = p_scr[:, 2 * hk + hv:2 * hk + 2 * hv]
    o_ref[0] = x + _dot((o_scr[...] * _silu(r)).astype(_BF16), wout_ref[...])

    @pl.when(t == n_t - 1)
    def _():
        sout_ref[0] = s_scr[...]


def _gla_call(x, s0, w, *, n_seq, tm, chunk):
    _, seq, d = x.shape
    n_h, dk, dv = s0.shape[1:]
    hk, hv = n_h * dk, n_h * dv
    shared = s0.shape[0] == 1 and n_seq > 1
    out, s_out = pl.pallas_call(
        functools.partial(_gla_kernel, chunk=chunk),
        grid=(n_seq, seq // tm),
        in_specs=[
            _tile_spec(tm, d),
            _init_spec(s0.shape, shared),
            _const_spec(w["gn"].shape), _const_spec(w["win"].shape),
            _const_spec(w["wgd"].shape), _const_spec(w["wgu"].shape),
            _const_spec(w["bgate"].shape), _const_spec(w["hg"].shape),
            _const_spec(w["wout"].shape),
        ],
        out_specs=[
            _tile_spec(tm, d),
            pl.BlockSpec((1, n_h, dk, dv), lambda b, t: (b, 0, 0, 0)),
        ],
        out_shape=[
            jax.ShapeDtypeStruct((n_seq, seq, d), _F32),
            jax.ShapeDtypeStruct((n_seq, n_h, dk, dv), _F32),
        ],
        scratch_shapes=[
            pltpu.VMEM((n_h, dk, dv), _F32),
            pltpu.VMEM((tm, 2 * hk + 2 * hv), _F32),
            pltpu.VMEM((tm, hk), _F32),
            pltpu.VMEM((tm, hv), _F32),
        ],
        compiler_params=_compiler_params(),
        name="gla_mixer",
    )(x, s0, w["gn"], w["win"], w["wgd"], w["wgu"], w["bgate"], w["hg"], w["wout"])
    return out, s_out


def _row(v):
    return v.reshape(1, -1).astype(_F32)


def _prep_gla(norm, w_in, w_gate_up, b_gate, head_gain, w_out):
    main = w_in.shape[1] - GLA_GATE_RANK
    wgd = jnp.pad(w_in[:, main:], ((0, 0), (0, LANES - GLA_GATE_RANK)))
    wgu = jnp.pad(w_gate_up, ((0, LANES - GLA_GATE_RANK), (0, 0)))
    return dict(gn=_row(norm), win=w_in[:, :main].astype(_BF16), wgd=wgd.astype(_BF16),
                wgu=wgu.astype(_BF16), bgate=_row(b_gate), hg=_row(head_gain),
                wout=w_out.astype(_BF16))


def _prep_sc(norm, w_in, conv_w, w_out):
    d = w_in.shape[0]
    win = w_in.reshape(d, 3, d).transpose(1, 0, 2).astype(_BF16)
    return dict(gn=_row(norm), win=win, cw=conv_w.astype(_F32), wout=w_out.astype(_BF16))


def _prep_ffn(norm, w_up, conv_w, conv_b, w_down):
    d, f2 = w_up.shape
    f = f2 // 2
    fc = FFN_COLS
    n_c = f // fc

    def cols(a):
        return a.reshape(a.shape[0], n_c, fc).transpose(1, 0, 2)

    return dict(gn=_row(norm),
                wg=cols(w_up[:, :f]).astype(_BF16), wu=cols(w_up[:, f:]).astype(_BF16),
                cwg=cols(conv_w[:, :f]).astype(_F32), cwu=cols(conv_w[:, f:]).astype(_F32),
                cbg=cols(conv_b[None, :f]).astype(_F32), cbu=cols(conv_b[None, f:]).astype(_F32),
                wd=w_down.reshape(n_c, fc, d).astype(_BF16))


def _tail_from_cache(cache):
    return jnp.pad(cache, ((0, 0), (SUBLANES - (CONV_W - 1), 0), (0, 0)))


def _ffn_tail_from_cache(cache, n_c, fc):
    b = cache.shape[0]
    t = _tail_from_cache(cache).reshape(b, SUBLANES, 2, n_c, fc)
    return t.transpose(0, 2, 3, 1, 4)


def _ffn_cache_from_tail(tail):
    b = tail.shape[0]
    rows = tail[:, :, :, SUBLANES - (CONV_W - 1):, :]
    return rows.transpose(0, 3, 1, 2, 4).reshape(b, CONV_W - 1, -1)


def _trunk(x, gla_s, conv_tail, ffn_tail, params, gfin, *, tm_gla, tm_sc, tm_ffn, chunk):
    n_seq = x.shape[0]
    depth = len(params)
    gla_new, conv_new, ffn_new = [], [], []
    i_gla = i_conv = 0
    for i, (mix, ffn) in enumerate(params):
        if i % 2 == 0:
            x, s = _gla_call(x, gla_s[i_gla], mix, n_seq=n_seq, tm=tm_gla, chunk=chunk)
            gla_new.append(s)
            i_gla += 1
        else:
            x, s = _sc_call(x, conv_tail[i_conv], mix, n_seq=n_seq, tm=tm_sc)
            conv_new.append(s)
            i_conv += 1
        x, s = _ffn_call(x, ffn_tail[i], ffn, gfin, n_seq=n_seq, tm=tm_ffn,
                         final_norm=(i == depth - 1))
        ffn_new.append(s)
    return x, gla_new, conv_new, ffn_new


def kernel(x_prompt, x_sample, state_gla, cache_conv, cache_ffn, meta, norm_mix, norm_ffn, norm_final,
           gla_w_in, gla_w_gate_up, gla_b_gate, gla_head_gain, gla_w_out,
           sc_w_in, sc_conv_w, sc_w_out, ffn_w_up, ffn_conv_w, ffn_conv_b, ffn_w_down):
    depth = norm_mix.shape[0]
    n_b, seq, d = x_prompt.shape
    n_s, s_len, _ = x_sample.shape
    assert s_len == N_META and seq % GLA_CHUNK == 0
    f = ffn_w_down.shape[1]
    n_c, fc = f // FFN_COLS, FFN_COLS

    params = []
    for i in range(depth):
        j = i // 2
        if i % 2 == 0:
            mix = _prep_gla(norm_mix[i], gla_w_in[j], gla_w_gate_up[j], gla_b_gate[j],
                            gla_head_gain[j], gla_w_out[j])
        else:
            mix = _prep_sc(norm_mix[i], sc_w_in[j], sc_conv_w[j], sc_w_out[j])
        params.append((mix, _prep_ffn(norm_ffn[i], ffn_w_up[i], ffn_conv_w[i], ffn_conv_b[i],
                                      ffn_w_down[i])))
    gfin = _row(norm_final)

    xs = jnp.concatenate([x_sample, meta[None].astype(x_sample.dtype)], axis=0)
    zero1 = lambda a: jnp.zeros((a.shape[0], 1) + a.shape[2:], a.dtype)
    gla_s = jnp.concatenate([state_gla, zero1(state_gla)], axis=1).astype(_F32)
    conv_c = jnp.concatenate([cache_conv, zero1(cache_conv)], axis=1).astype(_F32)
    ffn_c = jnp.concatenate([cache_ffn, zero1(cache_ffn)], axis=1).astype(_F32)
    ys, gla_1, conv_1, ffn_1 = _trunk(
        xs, [gla_s[j] for j in range(gla_s.shape[0])],
        [_tail_from_cache(conv_c[j]) for j in range(conv_c.shape[0])],
        [_ffn_tail_from_cache(ffn_c[i], n_c, fc) for i in range(depth)],
        params, gfin, tm_gla=s_len, tm_sc=s_len, tm_ffn=s_len, chunk=s_len)

    yp, gla_2, conv_2, ffn_2 = _trunk(
        x_prompt, [s[n_s:] for s in gla_1], [s[n_s:] for s in conv_1], [s[n_s:] for s in ffn_1],
        params, gfin, tm_gla=min(TILE_GLA, seq), tm_sc=min(TILE_SC, seq), tm_ffn=min(TILE_FFN, seq),
        chunk=GLA_CHUNK)

    dt = x_prompt.dtype
    tail2 = lambda s: s[:, SUBLANES - (CONV_W - 1):, :]
    return (yp.astype(dt), ys[:n_s].astype(dt),
            jnp.stack(gla_2).astype(dt), jnp.stack([s[:n_s] for s in gla_1]).astype(dt),
            jnp.stack([tail2(s) for s in conv_2]).astype(dt),
            jnp.stack([tail2(s[:n_s]) for s in conv_1]).astype(dt),
            jnp.stack([_ffn_cache_from_tail(s) for s in ffn_2]).astype(dt),
            jnp.stack([_ffn_cache_from_tail(s[:n_s]) for s in ffn_1]).astype(dt))
```

```python
import functools

import jax
import jax.numpy as jnp
from jax import lax
from jax.experimental import pallas as pl
from jax.experimental.pallas import tpu as pltpu

N_META = 16
GLA_HEADS = 4
GLA_GATE_RANK = 16
GLA_TAU = 16.0
CONV_W = 3
EPS = 1e-6
GLA_CHUNK = 64
MILD_LOG_DECAY = -60.0
FFN_COLS = 256
TILE_GLA = 256
TILE_SC = 512
TILE_FFN = 512
LANES = 128
SUBLANES = 8
VMEM_LIMIT_BYTES = 56 * 1024 * 1024

_BF16 = jnp.bfloat16
_F32 = jnp.float32
_LOG2E = 1.4426950408889634


def _dot(a, b):
    return lax.dot_general(a, b, (((1,), (0,)), ((), ())), preferred_element_type=_F32)


def _dot_nt(a, b):
    return lax.dot_general(a, b, (((1,), (1,)), ((), ())), preferred_element_type=_F32)


def _dot_tn(a, b):
    return lax.dot_general(a, b, (((0,), (0,)), ((), ())), preferred_element_type=_F32)


def _rmsnorm(x, g):
    return x * lax.rsqrt(jnp.mean(x * x, axis=-1, keepdims=True) + EPS) * g


def _silu(x):
    return x / (1.0 + jnp.exp2(x * (-_LOG2E)))


def _log_sigmoid(z):
    return jnp.minimum(z, 0.0) - jnp.log(1.0 + jnp.exp(-jnp.abs(z)))


def _shift_rows(z, tail, k):
    rolled = pltpu.roll(z, k, 0)
    head = rolled[:SUBLANES]
    rid = lax.broadcasted_iota(jnp.int32, head.shape, 0)
    head = jnp.where(rid < k, pltpu.roll(tail, k, 0), head)
    if z.shape[0] == SUBLANES:
        return head
    return jnp.concatenate([head, rolled[SUBLANES:]], axis=0)


def _causal_conv(z, tail, w):
    return w[2:3] * z + w[1:2] * _shift_rows(z, tail, 1) + w[0:1] * _shift_rows(z, tail, 2)


def _const_spec(shape):
    nd = len(shape)
    return pl.BlockSpec(shape, lambda b, t: (0,) * nd, pipeline_mode=pl.Buffered(1))


def _init_spec(shape, shared):
    nd = len(shape)
    block = (1,) + tuple(shape[1:])
    if shared:
        return pl.BlockSpec(block, lambda b, t: (0,) * nd)
    return pl.BlockSpec(block, lambda b, t: (b,) + (0,) * (nd - 1))


def _tile_spec(tm, d):
    return pl.BlockSpec((1, tm, d), lambda b, t: (b, t, 0))


def _next_tile_spec(tm, d, n_t):
    return pl.BlockSpec((1, tm, d), lambda b, t: (b, jnp.minimum(t + 1, n_t - 1), 0))


def _compiler_params():
    return pltpu.CompilerParams(dimension_semantics=("arbitrary", "arbitrary"),
                                vmem_limit_bytes=VMEM_LIMIT_BYTES)


def _ffn_kernel(x_ref, init_ref, gn_ref, wup_ref, cw_ref, cb_ref, wd_ref,
                gfin_ref, o_ref, cache_ref, zg_buf, zu_buf, *, final_norm):
    t = pl.program_id(1)
    n_t = pl.num_programs(1)
    n_c, _, fc = zg_buf.shape
    f = n_c * fc
    tm = x_ref.shape[1]
    s8 = SUBLANES

    @pl.when(t == 0)
    def _():
        zg_buf[:, 0:s8, :] = init_ref[0, 0]
        zu_buf[:, 0:s8, :] = init_ref[0, 1]

    x = x_ref[0]
    h = _rmsnorm(x, gn_ref[...]).astype(_BF16)

    def up(j):
        zg_buf[j, s8:s8 + tm, :] = _dot(h, wup_ref[:, j * fc:(j + 1) * fc])
        zu_buf[j, s8:s8 + tm, :] = _dot(h, wup_ref[:, f + j * fc:f + (j + 1) * fc])

    def conv(buf, j, c0):
        w = cw_ref[:, c0:c0 + fc]
        return (w[2:3] * buf[j, s8:s8 + tm, :] + w[1:2] * buf[j, s8 - 1:s8 - 1 + tm, :]
                + w[0:1] * buf[j, s8 - 2:s8 - 2 + tm, :] + cb_ref[:, c0:c0 + fc])

    def act(j):
        a = _silu(conv(zg_buf, j, j * fc)) * conv(zu_buf, j, f + j * fc)
        zg_buf[j, 0:s8, :] = zg_buf[j, tm:tm + s8, :]
        zu_buf[j, 0:s8, :] = zu_buf[j, tm:tm + s8, :]
        return a.astype(_BF16)

    up(0)
    if n_c > 1:
        up(1)
    a_prev = act(0)
    acc = x
    for j in range(1, n_c):
        a_cur = act(j)
        if j + 1 < n_c:
            up(j + 1)
        acc = acc + _dot(a_prev, wd_ref[(j - 1) * fc:j * fc, :])
        a_prev = a_cur
    acc = acc + _dot(a_prev, wd_ref[(n_c - 1) * fc:n_c * fc, :])
    if final_norm:
        acc = _rmsnorm(acc, gfin_ref[...])
    o_ref[0] = acc

    @pl.when(t == n_t - 1)
    def _():
        cache_ref[0, 0] = zg_buf[:, 0:s8, :]
        cache_ref[0, 1] = zu_buf[:, 0:s8, :]


def _ffn_call(x, init, w, gfin, *, n_seq, tm, final_norm):
    _, seq, d = x.shape
    fc = FFN_COLS
    n_c = w["wd"].shape[0] // fc
    shared = init.shape[0] == 1 and n_seq > 1
    tail_shape = (n_c, SUBLANES, fc)
    out, cache = pl.pallas_call(
        functools.partial(_ffn_kernel, final_norm=final_norm),
        grid=(n_seq, seq // tm),
        in_specs=[
            _tile_spec(tm, d),
            _init_spec(init.shape, shared),
            _const_spec(w["gn"].shape),
            _const_spec(w["wup"].shape), _const_spec(w["cw"].shape), _const_spec(w["cb"].shape),
            _const_spec(w["wd"].shape),
            _const_spec(gfin.shape),
        ],
        out_specs=[
            _tile_spec(tm, d),
            pl.BlockSpec((1, 2) + tail_shape, lambda b, t: (b, 0, 0, 0, 0)),
        ],
        out_shape=[
            jax.ShapeDtypeStruct((n_seq, seq, d), _F32),
            jax.ShapeDtypeStruct((n_seq, 2) + tail_shape, _F32),
        ],
        scratch_shapes=[pltpu.VMEM((n_c, SUBLANES + tm, fc), _F32),
                        pltpu.VMEM((n_c, SUBLANES + tm, fc), _F32)],
        compiler_params=_compiler_params(),
        name="conv_ffn",
    )(x, init, w["gn"], w["wup"], w["cw"], w["cb"], w["wd"], gfin)
    return out, cache


def _sc_kernel(x_ref, xn_ref, init_ref, gn_ref, win_ref, cw_ref, wout_ref, o_ref, cache_ref,
               tail, p_scr, *, pipelined):
    t = pl.program_id(1)
    n_t = pl.num_programs(1)
    tm = x_ref.shape[1]

    d = x_ref.shape[2]

    def project(h, slot):
        for i in range(3):
            p_scr[slot, i] = _dot(h, win_ref[:, i * d:(i + 1) * d])

    @pl.when(t == 0)
    def _():
        tail[...] = init_ref[0]
        project(_rmsnorm(x_ref[0], gn_ref[...]).astype(_BF16), 0)

    if pipelined:
        cur = lax.rem(t, 2)
        project(_rmsnorm(xn_ref[0], gn_ref[...]).astype(_BF16), 1 - cur)
    else:
        cur = 0
    ci = p_scr[cur, 1] * p_scr[cur, 2]
    y = _causal_conv(ci, tail[...], cw_ref[...])
    tail[...] = ci[tm - SUBLANES:]
    o_ref[0] = x_ref[0] + _dot((p_scr[cur, 0] * y).astype(_BF16), wout_ref[...])

    @pl.when(t == n_t - 1)
    def _():
        cache_ref[0] = tail[...]


def _sc_call(x, init, w, *, n_seq, tm):
    _, seq, d = x.shape
    n_t = seq // tm
    shared = init.shape[0] == 1 and n_seq > 1
    out, cache = pl.pallas_call(
        functools.partial(_sc_kernel, pipelined=n_t > 1),
        grid=(n_seq, n_t),
        in_specs=[
            _tile_spec(tm, d), _next_tile_spec(tm, d, n_t),
            _init_spec(init.shape, shared),
            _const_spec(w["gn"].shape), _const_spec(w["win"].shape),
            _const_spec(w["cw"].shape), _const_spec(w["wout"].shape),
        ],
        out_specs=[
            _tile_spec(tm, d),
            pl.BlockSpec((1, SUBLANES, d), lambda b, t: (b, 0, 0)),
        ],
        out_shape=[
            jax.ShapeDtypeStruct((n_seq, seq, d), _F32),
            jax.ShapeDtypeStruct((n_seq, SUBLANES, d), _F32),
        ],
        scratch_shapes=[pltpu.VMEM((SUBLANES, d), _F32), pltpu.VMEM((2, 3, tm, d), _F32)],
        compiler_params=_compiler_params(),
        name="shortconv_mixer",
    )(x, x, init, w["gn"], w["win"], w["cw"], w["wout"])
    return out, cache


def _cumsum_rows(a):
    n = a.shape[0]
    rid = lax.broadcasted_iota(jnp.int32, (SUBLANES, a.shape[1]), 0)
    groups = []
    for i in range(0, n, SUBLANES):
        grp = a[i:i + SUBLANES]
        s = 1
        while s < SUBLANES:
            grp = grp + jnp.where(rid >= s, pltpu.roll(grp, s, 0), 0.0)
            s *= 2
        if groups:
            grp = grp + groups[-1][SUBLANES - 1:SUBLANES]
        groups.append(grp)
    return jnp.concatenate(groups, axis=0) if len(groups) > 1 else groups[0]


def _level_masks(c):
    tt = lax.broadcasted_iota(jnp.int32, (c, c), 0)
    ss = lax.broadcasted_iota(jnp.int32, (c, c), 1)
    txs = tt ^ ss
    lower = tt > ss
    masks = [tt == ss]
    half = 1
    while half < c:
        masks.append(lower & (txs >= half) & (txs < 2 * half))
        half *= 2
    return masks


def _gla_chunk_scores(q, k, v, g, c, masks, mild):
    n_h = GLA_HEADS
    dk = q.shape[1] // n_h
    b = _cumsum_rows(g)
    b_last = b[c - 1:c]
    qe = (q * jnp.exp(b)).astype(_BF16)
    kd = (k * jnp.exp(b_last - b)).astype(_BF16)
    decay = jnp.exp(b_last)

    if mild:
        k_inv = (k * jnp.exp(-b)).astype(_BF16)
        tt = lax.broadcasted_iota(jnp.int32, (c, c), 0)
        causal = tt >= lax.broadcasted_iota(jnp.int32, (c, c), 1)
        scores = []
        for hh in range(n_h):
            ks = slice(hh * dk, (hh + 1) * dk)
            a = jnp.where(causal, _dot_nt(qe[:, ks], k_inv[:, ks]), 0.0)
            scores.append(a.astype(_BF16))
        return dict(scores=scores, qe=qe, kd=kd, decay=decay, v=v)

    rid = lax.broadcasted_iota(jnp.int32, b.shape, 0)
    levels = [(q.astype(_BF16), k.astype(_BF16))]
    e_k = b
    half = 1
    while half < c:
        hi = (rid & half) != 0
        b_mid = jnp.where(hi, pltpu.roll(e_k, half, 0), e_k)
        scale = jnp.exp2((b - b_mid) * jnp.where(hi, _LOG2E, -_LOG2E))
        x = (jnp.where(hi, q, k) * scale).astype(_BF16)
        levels.append((x, x))
        if 2 * half < c:
            e_k = jnp.where(hi, e_k, pltpu.roll(e_k, c - half, 0))
        half *= 2

    scores = []
    for hh in range(n_h):
        ks = slice(hh * dk, (hh + 1) * dk)
        a = jnp.zeros((c, c), _F32)
        for mask, (qs, kk) in zip(masks, levels):
            a = jnp.where(mask, _dot_nt(qs[:, ks], kk[:, ks]), a)
        scores.append(a.astype(_BF16))
    return dict(scores=scores, qe=qe, kd=kd, decay=decay, v=v)


def _gla_chunk_apply(pre, s_ref, hg):
    n_h = GLA_HEADS
    qe, kd, decay, v = pre["qe"], pre["kd"], pre["decay"], pre["v"]
    dk = qe.shape[1] // n_h
    dv = v.shape[1] // n_h
    outs = []
    for hh in range(n_h):
        ks = slice(hh * dk, (hh + 1) * dk)
        vs = slice(hh * dv, (hh + 1) * dv)
        s_old = s_ref[hh]
        o = _dot(qe[:, ks], s_old.astype(_BF16)) + _dot(pre["scores"][hh], v[:, vs])
        o = o * lax.rsqrt(jnp.mean(o * o, axis=-1, keepdims=True) + EPS) * hg[:, vs]
        outs.append(o)
        dcol = jnp.transpose(jnp.broadcast_to(decay[:, ks], (dk, dk)))
        dmat = jnp.concatenate([dcol] * (dv // dk), axis=1)
        s_ref[hh] = dmat * s_old + _dot_tn(kd[:, ks], v[:, vs])
    return jnp.concatenate(outs, axis=1)


def _gla_kernel(x_ref, s0_ref, gn_ref, win_ref, wgd_ref, wgu_ref, bgate_ref, hg_ref, wout_ref,
                o_ref, sout_ref, s_scr, p_scr, g_scr, o_scr, *, chunk):
    t = pl.program_id(1)
    n_t = pl.num_programs(1)
    tm = x_ref.shape[1]
    hk = g_scr.shape[1]
    hv = o_scr.shape[1]
    n_chunks = tm // chunk

    @pl.when(t == 0)
    def _():
        s_scr[...] = s0_ref[0]

    x = x_ref[0]
    h = _rmsnorm(x, gn_ref[...]).astype(_BF16)
    p_scr[...] = _dot(h, win_ref[...])
    gdown = _dot(h, wgd_ref[...]).astype(_BF16)
    z = _dot(gdown, wgu_ref[...]) + bgate_ref[...]
    g = _log_sigmoid(z) * (1.0 / GLA_TAU)
    g_scr[...] = g
    chunk_sums = [jnp.sum(g[ci * chunk:(ci + 1) * chunk], axis=0, keepdims=True)
                  for ci in range(n_chunks)]
    lowest = jnp.min(functools.reduce(jnp.minimum, chunk_sums))
    q_scale = float(hk // GLA_HEADS) ** -0.5
    masks = _level_masks(chunk)

    def run_chunks(mild):
        def scores(ci):
            rows = slice(ci * chunk, (ci + 1) * chunk)
            q = p_scr[rows, 0:hk] * q_scale
            k = p_scr[rows, hk:2 * hk]
            v = p_scr[rows, 2 * hk:2 * hk + hv].astype(_BF16)
            return _gla_chunk_scores(q, k, v, g_scr[rows, :], chunk, masks, mild)

        pre = scores(0)
        for ci in range(n_chunks):
            nxt_pre = scores(ci + 1) if ci + 1 < n_chunks else None
            o_scr[ci * chunk:(ci + 1) * chunk, :] = _gla_chunk_apply(pre, s_scr, hg_ref[...])
            pre = nxt_pre

    is_mild = lowest >= MILD_LOG_DECAY

    @pl.when(is_mild)
    def _():
        run_chunks(True)

    @pl.when(jnp.logical_not(is_mild))
    def _():
        run_chunks(False)

    r = p_scr[:, 2 * hk + hv:2 * hk + 2 * hv]
    o_ref[0] = x + _dot((o_scr[...] * _silu(r)).astype(_BF16), wout_ref[...])

    @pl.when(t == n_t - 1)
    def _():
        sout_ref[0] = s_scr[...]


def _gla_call(x, s0, w, *, n_seq, tm, chunk):
    _, seq, d = x.shape
    n_h, dk, dv = s0.shape[1:]
    hk, hv = n_h * dk, n_h * dv
    shared = s0.shape[0] == 1 and n_seq > 1
    out, s_out = pl.pallas_call(
        functools.partial(_gla_kernel, chunk=chunk),
        grid=(n_seq, seq // tm),
        in_specs=[
            _tile_spec(tm, d),
            _init_spec(s0.shape, shared),
            _const_spec(w["gn"].shape), _const_spec(w["win"].shape),
            _const_spec(w["wgd"].shape), _const_spec(w["wgu"].shape),
            _const_spec(w["bgate"].shape), _const_spec(w["hg"].shape),
            _const_spec(w["wout"].shape),
        ],
        out_specs=[
            _tile_spec(tm, d),
            pl.BlockSpec((1, n_h, dk, dv), lambda b, t: (b, 0, 0, 0)),
        ],
        out_shape=[
            jax.ShapeDtypeStruct((n_seq, seq, d), _F32),
            jax.ShapeDtypeStruct((n_seq, n_h, dk, dv), _F32),
        ],
        scratch_shapes=[
            pltpu.VMEM((n_h, dk, dv), _F32),
            pltpu.VMEM((tm, 2 * hk + 2 * hv), _F32),
            pltpu.VMEM((tm, hk), _F32),
            pltpu.VMEM((tm, hv), _F32),
        ],
        compiler_params=_compiler_params(),
        name="gla_mixer",
    )(x, s0, w["gn"], w["win"], w["wgd"], w["wgu"], w["bgate"], w["hg"], w["wout"])
    return out, s_out


SHORT_STRIDE = 32


def _short_spec(shape):
    nd = len(shape)
    return pl.BlockSpec(shape, lambda i: (0,) * nd, pipeline_mode=pl.Buffered(1))


def _short_params():
    return pltpu.CompilerParams(dimension_semantics=("arbitrary",), vmem_limit_bytes=VMEM_LIMIT_BYTES)


def _token_rows(n_rows, width, n_tok):
    pos = lax.broadcasted_iota(jnp.int32, (n_rows, width), 0) & (SHORT_STRIDE - 1)
    return pos >= SHORT_STRIDE - n_tok


def _roll_conv(z, w):
    return w[2:3] * z + w[1:2] * pltpu.roll(z, 1, 0) + w[0:1] * pltpu.roll(z, 2, 0)


def _last_rows(z, n_seg):
    return [z[(s + 1) * SHORT_STRIDE - SUBLANES:(s + 1) * SHORT_STRIDE] for s in range(n_seg)]


def _ffn_short_kernel(x_ref, inj_ref, gn_ref, wup_ref, cw_ref, cb_ref, wd_ref, gfin_ref,
                      o_ref, tail_ref, *, final_norm, n_tok):
    n_seg = tail_ref.shape[0]
    f = wd_ref.shape[0]
    fc = FFN_COLS
    x = x_ref[...]
    h = _rmsnorm(x, gn_ref[...]).astype(_BF16)
    acc = x
    for j in range(f // fc):
        halves = []
        for c0 in (j * fc, f + j * fc):
            cols = slice(c0, c0 + fc)
            z = _dot(h, wup_ref[:, cols]) + inj_ref[:, cols]
            for s, rows in enumerate(_last_rows(z, n_seg)):
                tail_ref[s, :, cols] = rows
            halves.append(_roll_conv(z, cw_ref[:, cols]) + cb_ref[:, cols])
        a = (_silu(halves[0]) * halves[1]).astype(_BF16)
        acc = acc + _dot(a, wd_ref[j * fc:(j + 1) * fc, :])
    if final_norm:
        acc = _rmsnorm(acc, gfin_ref[...])
    o_ref[...] = jnp.where(_token_rows(x.shape[0], x.shape[1], n_tok), acc, 0.0)


def _ffn_short_call(x, inj, w, gfin, *, n_seg, n_tok, final_norm):
    rows, d = x.shape
    f2 = w["wup"].shape[1]
    args = (x, inj, w["gn"], w["wup"], w["cw"], w["cb"], w["wd"], gfin)
    return pl.pallas_call(
        functools.partial(_ffn_short_kernel, final_norm=final_norm, n_tok=n_tok),
        grid=(1,),
        in_specs=[_short_spec(a.shape) for a in args],
        out_specs=[_short_spec((rows, d)), _short_spec((n_seg, SUBLANES, f2))],
        out_shape=[jax.ShapeDtypeStruct((rows, d), _F32),
                   jax.ShapeDtypeStruct((n_seg, SUBLANES, f2), _F32)],
        compiler_params=_short_params(),
        name="conv_ffn_short",
    )(*args)


def _sc_short_kernel(x_ref, inj_ref, gn_ref, win_ref, cw_ref, wout_ref, o_ref, tail_ref, *, n_tok):
    n_seg = tail_ref.shape[0]
    x = x_ref[...]
    d = x.shape[1]
    h = _rmsnorm(x, gn_ref[...]).astype(_BF16)
    bg = _dot(h, win_ref[:, 0:d])
    ci = _dot(h, win_ref[:, d:2 * d]) * _dot(h, win_ref[:, 2 * d:3 * d]) + inj_ref[...]
    for s, rows in enumerate(_last_rows(ci, n_seg)):
        tail_ref[s] = rows
    y = _roll_conv(ci, cw_ref[...])
    out = x + _dot((bg * y).astype(_BF16), wout_ref[...])
    o_ref[...] = jnp.where(_token_rows(x.shape[0], d, n_tok), out, 0.0)


def _sc_short_call(x, inj, w, *, n_seg, n_tok):
    rows, d = x.shape
    args = (x, inj, w["gn"], w["win"], w["cw"], w["wout"])
    return pl.pallas_call(
        functools.partial(_sc_short_kernel, n_tok=n_tok),
        grid=(1,),
        in_specs=[_short_spec(a.shape) for a in args],
        out_specs=[_short_spec((rows, d)), _short_spec((n_seg, SUBLANES, d))],
        out_shape=[jax.ShapeDtypeStruct((rows, d), _F32),
                   jax.ShapeDtypeStruct((n_seg, SUBLANES, d), _F32)],
        compiler_params=_short_params(),
        name="shortconv_mixer_short",
    )(*args)


def _gla_short_kernel(x_ref, s0_ref, gn_ref, win_ref, wgd_ref, wgu_ref, bgate_ref, hg_ref, wout_ref,
                      o_ref, sout_ref, s_scr, p_scr, g_scr, o_scr, *, n_tok):
    n_seg = s0_ref.shape[0]
    c = SHORT_STRIDE
    hk = g_scr.shape[1]
    hv = o_scr.shape[1]
    x = x_ref[...]
    h = _rmsnorm(x, gn_ref[...]).astype(_BF16)
    p_scr[...] = _dot(h, win_ref[...])
    gdown = _dot(h, wgd_ref[...]).astype(_BF16)
    z = _dot(gdown, wgu_ref[...]) + bgate_ref[...]
    g = jnp.where(_token_rows(x.shape[0], hk, n_tok), _log_sigmoid(z) * (1.0 / GLA_TAU), 0.0)
    g_scr[...] = g
    sums = [jnp.sum(g[s * c:(s + 1) * c], axis=0, keepdims=True) for s in range(n_seg)]
    is_mild = jnp.min(functools.reduce(jnp.minimum, sums)) >= MILD_LOG_DECAY
    q_scale = float(hk // GLA_HEADS) ** -0.5
    masks = _level_masks(c)

    def walk(mild):
        def body(s, carry):
            rows = pl.ds(pl.multiple_of(s * c, c), c)
            q = p_scr[rows, 0:hk] * q_scale
            k = p_scr[rows, hk:2 * hk]
            v = p_scr[rows, 2 * hk:2 * hk + hv].astype(_BF16)
            s_scr[...] = s0_ref[s]
            pre = _gla_chunk_scores(q, k, v, g_scr[rows, :], c, masks, mild)
            o_scr[rows, :] = _gla_chunk_apply(pre, s_scr, hg_ref[...])
            sout_ref[s] = s_scr[...]
            return carry

        lax.fori_loop(0, n_seg, body, 0)

    @pl.when(is_mild)
    def _():
        walk(True)

    @pl.when(jnp.logical_not(is_mild))
    def _():
        walk(False)

    r = p_scr[:, 2 * hk + hv:2 * hk + 2 * hv]
    out = x + _dot((o_scr[...] * _silu(r)).astype(_BF16), wout_ref[...])
    o_ref[...] = jnp.where(_token_rows(x.shape[0], x.shape[1], n_tok), out, 0.0)


def _gla_short_call(x, s0, w, *, n_tok):
    rows, d = x.shape
    n_seg, n_h, dk, dv = s0.shape
    hk, hv = n_h * dk, n_h * dv
    args = (x, s0, w["gn"], w["win"], w["wgd"], w["wgu"], w["bgate"], w["hg"], w["wout"])
    return pl.pallas_call(
        functools.partial(_gla_short_kernel, n_tok=n_tok),
        grid=(1,),
        in_specs=[_short_spec(a.shape) for a in args],
        out_specs=[_short_spec((rows, d)), _short_spec(s0.shape)],
        out_shape=[jax.ShapeDtypeStruct((rows, d), _F32), jax.ShapeDtypeStruct(s0.shape, _F32)],
        scratch_shapes=[
            pltpu.VMEM((n_h, dk, dv), _F32),
            pltpu.VMEM((rows, 2 * hk + 2 * hv), _F32),
            pltpu.VMEM((rows, hk), _F32),
            pltpu.VMEM((rows, hv), _F32),
        ],
        compiler_params=_short_params(),
        name="gla_mixer_short",
    )(*args)


def _short_rows(a):
    n_seg, n_tok, width = a.shape
    return jnp.pad(a, ((0, 0), (SHORT_STRIDE - n_tok, 0), (0, 0))).reshape(n_seg * SHORT_STRIDE, width)


def _short_inject(cache, n_tok):
    n_seg, n_prev, width = cache.shape
    lead = SHORT_STRIDE - n_tok - n_prev
    return jnp.pad(cache, ((0, 0), (lead, n_tok), (0, 0))).reshape(n_seg * SHORT_STRIDE, width)


def _short_trunk(xs, gla_s, conv_c, ffn_c, params, gfin):
    n_seg, n_tok, d = xs.shape
    depth = len(params)
    x = _short_rows(xs)
    keep = lambda tail: tail[:, SUBLANES - (CONV_W - 1):, :]
    gla_new, conv_new, ffn_new = [], [], []
    for i, (mix, ffn) in enumerate(params):
        if i % 2 == 0:
            x, s = _gla_short_call(x, gla_s[i // 2], mix, n_tok=n_tok)
            gla_new.append(s)
        else:
            x, tail = _sc_short_call(x, _short_inject(conv_c[i // 2], n_tok), mix,
                                     n_seg=n_seg, n_tok=n_tok)
            conv_new.append(keep(tail))
        x, tail = _ffn_short_call(x, _short_inject(ffn_c[i], n_tok), ffn, gfin, n_seg=n_seg,
                                  n_tok=n_tok, final_norm=(i == depth - 1))
        ffn_new.append(keep(tail))
    y = x.reshape(n_seg, SHORT_STRIDE, d)[:, SHORT_STRIDE - n_tok:, :]
    return y, gla_new, conv_new, ffn_new


def _row(v):
    return v.reshape(1, -1).astype(_F32)


def _prep_gla(norm, w_in, w_gate_up, b_gate, head_gain, w_out):
    main = w_in.shape[1] - GLA_GATE_RANK
    wgd = jnp.pad(w_in[:, main:], ((0, 0), (0, LANES - GLA_GATE_RANK)))
    wgu = jnp.pad(w_gate_up, ((0, LANES - GLA_GATE_RANK), (0, 0)))
    return dict(gn=_row(norm), win=w_in[:, :main].astype(_BF16), wgd=wgd.astype(_BF16),
                wgu=wgu.astype(_BF16), bgate=_row(b_gate), hg=_row(head_gain),
                wout=w_out.astype(_BF16))


def _prep_sc(norm, w_in, conv_w, w_out):
    return dict(gn=_row(norm), win=w_in.astype(_BF16), cw=conv_w.astype(_F32),
                wout=w_out.astype(_BF16))


def _prep_ffn(norm, w_up, conv_w, conv_b, w_down):
    assert w_down.shape[0] % FFN_COLS == 0
    return dict(gn=_row(norm), wup=w_up.astype(_BF16), cw=conv_w.astype(_F32), cb=_row(conv_b),
                wd=w_down.astype(_BF16))


def _tail_from_cache(cache):
    return jnp.pad(cache, ((0, 0), (SUBLANES - (CONV_W - 1), 0), (0, 0)))


def _ffn_tail_from_cache(cache, n_c, fc):
    b = cache.shape[0]
    t = _tail_from_cache(cache).reshape(b, SUBLANES, 2, n_c, fc)
    return t.transpose(0, 2, 3, 1, 4)


def _ffn_cache_from_tail(tail):
    b = tail.shape[0]
    rows = tail[:, :, :, SUBLANES - (CONV_W - 1):, :]
    return rows.transpose(0, 3, 1, 2, 4).reshape(b, CONV_W - 1, -1)


def _trunk(x, gla_s, conv_tail, ffn_tail, params, gfin, *, tm_gla, tm_sc, tm_ffn, chunk):
    n_seq = x.shape[0]
    depth = len(params)
    gla_new, conv_new, ffn_new = [], [], []
    i_gla = i_conv = 0
    for i, (mix, ffn) in enumerate(params):
        if i % 2 == 0:
            x, s = _gla_call(x, gla_s[i_gla], mix, n_seq=n_seq, tm=tm_gla, chunk=chunk)
            gla_new.append(s)
            i_gla += 1
        else:
            x, s = _sc_call(x, conv_tail[i_conv], mix, n_seq=n_seq, tm=tm_sc)
            conv_new.append(s)
            i_conv += 1
        x, s = _ffn_call(x, ffn_tail[i], ffn, gfin, n_seq=n_seq, tm=tm_ffn,
                         final_norm=(i == depth - 1))
        ffn_new.append(s)
    return x, gla_new, conv_new, ffn_new


def kernel(x_prompt, x_sample, state_gla, cache_conv, cache_ffn, meta, norm_mix, norm_ffn, norm_final,
           gla_w_in, gla_w_gate_up, gla_b_gate, gla_head_gain, gla_w_out,
           sc_w_in, sc_conv_w, sc_w_out, ffn_w_up, ffn_conv_w, ffn_conv_b, ffn_w_down):
    depth = norm_mix.shape[0]
    n_b, seq, d = x_prompt.shape
    n_s, s_len, _ = x_sample.shape
    assert s_len == N_META and seq % GLA_CHUNK == 0
    f = ffn_w_down.shape[1]
    n_c, fc = f // FFN_COLS, FFN_COLS

    params = []
    for i in range(depth):
        j = i // 2
        if i % 2 == 0:
            mix = _prep_gla(norm_mix[i], gla_w_in[j], gla_w_gate_up[j], gla_b_gate[j],
                            gla_head_gain[j], gla_w_out[j])
        else:
            mix = _prep_sc(norm_mix[i], sc_w_in[j], sc_conv_w[j], sc_w_out[j])
        params.append((mix, _prep_ffn(norm_ffn[i], ffn_w_up[i], ffn_conv_w[i], ffn_conv_b[i],
                                      ffn_w_down[i])))
    gfin = _row(norm_final)

    xs = jnp.concatenate([x_sample, meta[None].astype(x_sample.dtype)], axis=0)
    zero1 = lambda a: jnp.zeros((a.shape[0], 1) + a.shape[2:], a.dtype)
    gla_s = jnp.concatenate([state_gla, zero1(state_gla)], axis=1).astype(_F32)
    conv_c = jnp.concatenate([cache_conv, zero1(cache_conv)], axis=1).astype(_F32)
    ffn_c = jnp.concatenate([cache_ffn, zero1(cache_ffn)], axis=1).astype(_F32)
    ys, gla_1, conv_1, ffn_1 = _short_trunk(xs, gla_s, conv_c, ffn_c, params, gfin)

    yp, gla_2, conv_2, ffn_2 = _trunk(
        x_prompt, [s[n_s:] for s in gla_1],
        [_tail_from_cache(s[n_s:]) for s in conv_1],
        [_ffn_tail_from_cache(s[n_s:], n_c, fc) for s in ffn_1],
        params, gfin, tm_gla=min(TILE_GLA, seq), tm_sc=min(TILE_SC, seq), tm_ffn=min(TILE_FFN, seq),
        chunk=GLA_CHUNK)

    dt = x_prompt.dtype
    tail2 = lambda s: s[:, SUBLANES - (CONV_W - 1):, :]
    return (yp.astype(dt), ys[:n_s].astype(dt),
            jnp.stack(gla_2).astype(dt), jnp.stack([s[:n_s] for s in gla_1]).astype(dt),
            jnp.stack([tail2(s) for s in conv_2]).astype(dt),
            jnp.stack([s[:n_s] for s in conv_1]).astype(dt),
            jnp.stack([_ffn_cache_from_tail(s) for s in ffn_2]).astype(dt),
            jnp.stack([s[:n_s] for s in ffn_1]).astype(dt))
```

```python
import functools

import jax
import jax.numpy as jnp
from jax import lax
from jax.experimental import pallas as pl
from jax.experimental.pallas import tpu as pltpu

N_META = 16
GLA_HEADS = 4
GLA_GATE_RANK = 16
GLA_TAU = 16.0
CONV_W = 3
EPS = 1e-6
GLA_CHUNK = 64
MILD_LOG_DECAY = -60.0
FFN_COLS = 256
TILE_GLA = 256
TILE_SC = 512
TILE_FFN = 512
LANES = 128
SUBLANES = 8
VMEM_LIMIT_BYTES = 56 * 1024 * 1024

_BF16 = jnp.bfloat16
_F32 = jnp.float32
_LOG2E = 1.4426950408889634


def _dot(a, b):
    return lax.dot_general(a, b, (((1,), (0,)), ((), ())), preferred_element_type=_F32)


def _dot_nt(a, b):
    return lax.dot_general(a, b, (((1,), (1,)), ((), ())), preferred_element_type=_F32)


def _dot_tn(a, b):
    return lax.dot_general(a, b, (((0,), (0,)), ((), ())), preferred_element_type=_F32)


def _rmsnorm(x, g):
    return x * lax.rsqrt(jnp.mean(x * x, axis=-1, keepdims=True) + EPS) * g


def _silu(x):
    return x / (1.0 + jnp.exp2(x * (-_LOG2E)))


def _log_sigmoid(z):
    return jnp.minimum(z, 0.0) - jnp.log(1.0 + jnp.exp(-jnp.abs(z)))


def _shift_rows(z, tail, k):
    rolled = pltpu.roll(z, k, 0)
    head = rolled[:SUBLANES]
    rid = lax.broadcasted_iota(jnp.int32, head.shape, 0)
    head = jnp.where(rid < k, pltpu.roll(tail, k, 0), head)
    if z.shape[0] == SUBLANES:
        return head
    return jnp.concatenate([head, rolled[SUBLANES:]], axis=0)


def _causal_conv(z, tail, w):
    return w[2:3] * z + w[1:2] * _shift_rows(z, tail, 1) + w[0:1] * _shift_rows(z, tail, 2)


def _const_spec(shape):
    nd = len(shape)
    return pl.BlockSpec(shape, lambda b, t: (0,) * nd, pipeline_mode=pl.Buffered(1))


def _init_spec(shape, shared):
    nd = len(shape)
    block = (1,) + tuple(shape[1:])
    if shared:
        return pl.BlockSpec(block, lambda b, t: (0,) * nd)
    return pl.BlockSpec(block, lambda b, t: (b,) + (0,) * (nd - 1))


def _tile_spec(tm, d):
    return pl.BlockSpec((1, tm, d), lambda b, t: (b, t, 0))


def _next_tile_spec(tm, d, n_t):
    return pl.BlockSpec((1, tm, d), lambda b, t: (b, jnp.minimum(t + 1, n_t - 1), 0))


def _compiler_params():
    return pltpu.CompilerParams(dimension_semantics=("arbitrary", "arbitrary"),
                                vmem_limit_bytes=VMEM_LIMIT_BYTES)


def _ffn_kernel(x_ref, init_ref, gn_ref, wup_ref, cw_ref, cb_ref, wd_ref,
                gfin_ref, o_ref, cache_ref, zg_buf, zu_buf, *, final_norm):
    t = pl.program_id(1)
    n_t = pl.num_programs(1)
    n_c, _, fc = zg_buf.shape
    f = n_c * fc
    tm = x_ref.shape[1]
    s8 = SUBLANES

    @pl.when(t == 0)
    def _():
        zg_buf[:, 0:s8, :] = init_ref[0, 0]
        zu_buf[:, 0:s8, :] = init_ref[0, 1]

    x = x_ref[0]
    h = _rmsnorm(x, gn_ref[...]).astype(_BF16)

    def up(j):
        zg_buf[j, s8:s8 + tm, :] = _dot(h, wup_ref[:, j * fc:(j + 1) * fc])
        zu_buf[j, s8:s8 + tm, :] = _dot(h, wup_ref[:, f + j * fc:f + (j + 1) * fc])

    def conv(buf, j, c0):
        w = cw_ref[:, c0:c0 + fc]
        return (w[2:3] * buf[j, s8:s8 + tm, :] + w[1:2] * buf[j, s8 - 1:s8 - 1 + tm, :]
                + w[0:1] * buf[j, s8 - 2:s8 - 2 + tm, :] + cb_ref[:, c0:c0 + fc])

    def act(j):
        a = _silu(conv(zg_buf, j, j * fc)) * conv(zu_buf, j, f + j * fc)
        zg_buf[j, 0:s8, :] = zg_buf[j, tm:tm + s8, :]
        zu_buf[j, 0:s8, :] = zu_buf[j, tm:tm + s8, :]
        return a.astype(_BF16)

    for j in range(n_c):
        up(j)
    acts = [act(j) for j in range(n_c)]
    acc = x
    for j in range(n_c):
        acc = acc + _dot(acts[j], wd_ref[j * fc:(j + 1) * fc, :])
    if final_norm:
        acc = _rmsnorm(acc, gfin_ref[...])
    o_ref[0] = acc

    @pl.when(t == n_t - 1)
    def _():
        cache_ref[0, 0] = zg_buf[:, 0:s8, :]
        cache_ref[0, 1] = zu_buf[:, 0:s8, :]


def _ffn_call(x, init, w, gfin, *, n_seq, tm, final_norm):
    _, seq, d = x.shape
    fc = FFN_COLS
    n_c = w["wd"].shape[0] // fc
    shared = init.shape[0] == 1 and n_seq > 1
    tail_shape = (n_c, SUBLANES, fc)
    out, cache = pl.pallas_call(
        functools.partial(_ffn_kernel, final_norm=final_norm),
        grid=(n_seq, seq // tm),
        in_specs=[
            _tile_spec(tm, d),
            _init_spec(init.shape, shared),
            _const_spec(w["gn"].shape),
            _const_spec(w["wup"].shape), _const_spec(w["cw"].shape), _const_spec(w["cb"].shape),
            _const_spec(w["wd"].shape),
            _const_spec(gfin.shape),
        ],
        out_specs=[
            _tile_spec(tm, d),
            pl.BlockSpec((1, 2) + tail_shape, lambda b, t: (b, 0, 0, 0, 0)),
        ],
        out_shape=[
            jax.ShapeDtypeStruct((n_seq, seq, d), _F32),
            jax.ShapeDtypeStruct((n_seq, 2) + tail_shape, _F32),
        ],
        scratch_shapes=[pltpu.VMEM((n_c, SUBLANES + tm, fc), _F32),
                        pltpu.VMEM((n_c, SUBLANES + tm, fc), _F32)],
        compiler_params=_compiler_params(),
        name="conv_ffn",
    )(x, init, w["gn"], w["wup"], w["cw"], w["cb"], w["wd"], gfin)
    return out, cache


def _sc_kernel(x_ref, xn_ref, init_ref, gn_ref, win_ref, cw_ref, wout_ref, o_ref, cache_ref,
               tail, p_scr, *, pipelined):
    t = pl.program_id(1)
    n_t = pl.num_programs(1)
    tm = x_ref.shape[1]

    d = x_ref.shape[2]

    def project(h, slot):
        for i in range(3):
            p_scr[slot, i] = _dot(h, win_ref[:, i * d:(i + 1) * d])

    @pl.when(t == 0)
    def _():
        tail[...] = init_ref[0]
        project(_rmsnorm(x_ref[0], gn_ref[...]).astype(_BF16), 0)

    if pipelined:
        cur = lax.rem(t, 2)
        project(_rmsnorm(xn_ref[0], gn_ref[...]).astype(_BF16), 1 - cur)
    else:
        cur = 0
    ci = p_scr[cur, 1] * p_scr[cur, 2]
    y = _causal_conv(ci, tail[...], cw_ref[...])
    tail[...] = ci[tm - SUBLANES:]
    o_ref[0] = x_ref[0] + _dot((p_scr[cur, 0] * y).astype(_BF16), wout_ref[...])

    @pl.when(t == n_t - 1)
    def _():
        cache_ref[0] = tail[...]


def _sc_call(x, init, w, *, n_seq, tm):
    _, seq, d = x.shape
    n_t = seq // tm
    shared = init.shape[0] == 1 and n_seq > 1
    out, cache = pl.pallas_call(
        functools.partial(_sc_kernel, pipelined=n_t > 1),
        grid=(n_seq, n_t),
        in_specs=[
            _tile_spec(tm, d), _next_tile_spec(tm, d, n_t),
            _init_spec(init.shape, shared),
            _const_spec(w["gn"].shape), _const_spec(w["win"].shape),
            _const_spec(w["cw"].shape), _const_spec(w["wout"].shape),
        ],
        out_specs=[
            _tile_spec(tm, d),
            pl.BlockSpec((1, SUBLANES, d), lambda b, t: (b, 0, 0)),
        ],
        out_shape=[
            jax.ShapeDtypeStruct((n_seq, seq, d), _F32),
            jax.ShapeDtypeStruct((n_seq, SUBLANES, d), _F32),
        ],
        scratch_shapes=[pltpu.VMEM((SUBLANES, d), _F32), pltpu.VMEM((2, 3, tm, d), _F32)],
        compiler_params=_compiler_params(),
        name="shortconv_mixer",
    )(x, x, init, w["gn"], w["win"], w["cw"], w["wout"])
    return out, cache


def _cumsum_rows(a):
    n = a.shape[0]
    rid = lax.broadcasted_iota(jnp.int32, (SUBLANES, a.shape[1]), 0)
    groups = []
    for i in range(0, n, SUBLANES):
        grp = a[i:i + SUBLANES]
        s = 1
        while s < SUBLANES:
            grp = grp + jnp.where(rid >= s, pltpu.roll(grp, s, 0), 0.0)
            s *= 2
        if groups:
            grp = grp + groups[-1][SUBLANES - 1:SUBLANES]
        groups.append(grp)
    return jnp.concatenate(groups, axis=0) if len(groups) > 1 else groups[0]


def _level_masks(c):
    tt = lax.broadcasted_iota(jnp.int32, (c, c), 0)
    ss = lax.broadcasted_iota(jnp.int32, (c, c), 1)
    txs = tt ^ ss
    lower = tt > ss
    masks = [tt == ss]
    half = 1
    while half < c:
        masks.append(lower & (txs >= half) & (txs < 2 * half))
        half *= 2
    return masks


def _gla_chunk_scores(q, k, v, g, c, masks, mild):
    n_h = GLA_HEADS
    dk = q.shape[1] // n_h
    b = _cumsum_rows(g)
    b_last = b[c - 1:c]
    qe = (q * jnp.exp(b)).astype(_BF16)
    kd = (k * jnp.exp(b_last - b)).astype(_BF16)
    decay = jnp.exp(b_last)

    if mild:
        k_inv = (k * jnp.exp(-b)).astype(_BF16)
        tt = lax.broadcasted_iota(jnp.int32, (c, c), 0)
        causal = tt >= lax.broadcasted_iota(jnp.int32, (c, c), 1)
        scores = []
        for hh in range(n_h):
            ks = slice(hh * dk, (hh + 1) * dk)
            a = jnp.where(causal, _dot_nt(qe[:, ks], k_inv[:, ks]), 0.0)
            scores.append(a.astype(_BF16))
        return dict(scores=scores, qe=qe, kd=kd, decay=decay, v=v)

    rid = lax.broadcasted_iota(jnp.int32, b.shape, 0)
    levels = [(q.astype(_BF16), k.astype(_BF16))]
    e_k = b
    half = 1
    while half < c:
        hi = (rid & half) != 0
        b_mid = jnp.where(hi, pltpu.roll(e_k, half, 0), e_k)
        scale = jnp.exp2((b - b_mid) * jnp.where(hi, _LOG2E, -_LOG2E))
        x = (jnp.where(hi, q, k) * scale).astype(_BF16)
        levels.append((x, x))
        if 2 * half < c:
            e_k = jnp.where(hi, e_k, pltpu.roll(e_k, c - half, 0))
        half *= 2

    scores = []
    for hh in range(n_h):
        ks = slice(hh * dk, (hh + 1) * dk)
        a = jnp.zeros((c, c), _F32)
        for mask, (qs, kk) in zip(masks, levels):
            a = jnp.where(mask, _dot_nt(qs[:, ks], kk[:, ks]), a)
        scores.append(a.astype(_BF16))
    return dict(scores=scores, qe=qe, kd=kd, decay=decay, v=v)


def _gla_chunk_apply(pre, s_ref, hg):
    n_h = GLA_HEADS
    qe, kd, decay, v = pre["qe"], pre["kd"], pre["decay"], pre["v"]
    dk = qe.shape[1] // n_h
    dv = v.shape[1] // n_h
    outs = []
    for hh in range(n_h):
        ks = slice(hh * dk, (hh + 1) * dk)
        vs = slice(hh * dv, (hh + 1) * dv)
        s_old = s_ref[hh]
        o = _dot(qe[:, ks], s_old.astype(_BF16)) + _dot(pre["scores"][hh], v[:, vs])
        o = o * lax.rsqrt(jnp.mean(o * o, axis=-1, keepdims=True) + EPS) * hg[:, vs]
        outs.append(o)
        dcol = jnp.transpose(jnp.broadcast_to(decay[:, ks], (dk, dk)))
        dmat = jnp.concatenate([dcol] * (dv // dk), axis=1)
        s_ref[hh] = dmat * s_old + _dot_tn(kd[:, ks], v[:, vs])
    return jnp.concatenate(outs, axis=1)


def _gla_kernel(x_ref, s0_ref, gn_ref, win_ref, wgd_ref, wgu_ref, bgate_ref, hg_ref, wout_ref,
                o_ref, sout_ref, s_scr, p_scr, g_scr, o_scr, *, chunk):
    t = pl.program_id(1)
    n_t = pl.num_programs(1)
    tm = x_ref.shape[1]
    hk = g_scr.shape[1]
    hv = o_scr.shape[1]
    n_chunks = tm // chunk

    @pl.when(t == 0)
    def _():
        s_scr[...] = s0_ref[0]

    x = x_ref[0]
    h = _rmsnorm(x, gn_ref[...]).astype(_BF16)
    p_scr[...] = _dot(h, win_ref[...])
    gdown = _dot(h, wgd_ref[...]).astype(_BF16)
    z = _dot(gdown, wgu_ref[...]) + bgate_ref[...]
    g = _log_sigmoid(z) * (1.0 / GLA_TAU)
    g_scr[...] = g
    chunk_sums = [jnp.sum(g[ci * chunk:(ci + 1) * chunk], axis=0, keepdims=True)
                  for ci in range(n_chunks)]
    lowest = jnp.min(functools.reduce(jnp.minimum, chunk_sums))
    q_scale = float(hk // GLA_HEADS) ** -0.5
    masks = _level_masks(chunk)

    def run_chunks(mild):
        def scores(ci):
            rows = slice(ci * chunk, (ci + 1) * chunk)
            q = p_scr[rows, 0:hk] * q_scale
            k = p_scr[rows, hk:2 * hk]
            v = p_scr[rows, 2 * hk:2 * hk + hv].astype(_BF16)
            return _gla_chunk_scores(q, k, v, g_scr[rows, :], chunk, masks, mild)

        pre = scores(0)
        for ci in range(n_chunks):
            nxt_pre = scores(ci + 1) if ci + 1 < n_chunks else None
            o_scr[ci * chunk:(ci + 1) * chunk, :] = _gla_chunk_apply(pre, s_scr, hg_ref[...])
            pre = nxt_pre

    is_mild = lowest >= MILD_LOG_DECAY

    @pl.when(is_mild)
    def _():
        run_chunks(True)

    @pl.when(jnp.logical_not(is_mild))
    def _():
        run_chunks(False)

    r = p_scr[:, 2 * hk + hv:2 * hk + 2 * hv]
    o_ref[0] = x + _dot((o_scr[...] * _silu(r)).astype(_BF16), wout_ref[...])

    @pl.when(t == n_t - 1)
    def _():
        sout_ref[0] = s_scr[...]


def _gla_call(x, s0, w, *, n_seq, tm, chunk):
    _, seq, d = x.shape
    n_h, dk, dv = s0.shape[1:]
    hk, hv = n_h * dk, n_h * dv
    shared = s0.shape[0] == 1 and n_seq > 1
    out, s_out = pl.pallas_call(
        functools.partial(_gla_kernel, chunk=chunk),
        grid=(n_seq, seq // tm),
        in_specs=[
            _tile_spec(tm, d),
            _init_spec(s0.shape, shared),
            _const_spec(w["gn"].shape), _const_spec(w["win"].shape),
            _const_spec(w["wgd"].shape), _const_spec(w["wgu"].shape),
            _const_spec(w["bgate"].shape), _const_spec(w["hg"].shape),
            _const_spec(w["wout"].shape),
        ],
        out_specs=[
            _tile_spec(tm, d),
            pl.BlockSpec((1, n_h, dk, dv), lambda b, t: (b, 0, 0, 0)),
        ],
        out_shape=[
            jax.ShapeDtypeStruct((n_seq, seq, d), _F32),
            jax.ShapeDtypeStruct((n_seq, n_h, dk, dv), _F32),
        ],
        scratch_shapes=[
            pltpu.VMEM((n_h, dk, dv), _F32),
            pltpu.VMEM((tm, 2 * hk + 2 * hv), _F32),
            pltpu.VMEM((tm, hk), _F32),
            pltpu.VMEM((tm, hv), _F32),
        ],
        compiler_params=_compiler_params(),
        name="gla_mixer",
    )(x, s0, w["gn"], w["win"], w["wgd"], w["wgu"], w["bgate"], w["hg"], w["wout"])
    return out, s_out


SHORT_STRIDE = 32


def _short_spec(shape):
    nd = len(shape)
    return pl.BlockSpec(shape, lambda i: (0,) * nd, pipeline_mode=pl.Buffered(1))


def _short_params():
    return pltpu.CompilerParams(dimension_semantics=("arbitrary",), vmem_limit_bytes=VMEM_LIMIT_BYTES)


def _token_rows(n_rows, width, n_tok):
    pos = lax.broadcasted_iota(jnp.int32, (n_rows, width), 0) & (SHORT_STRIDE - 1)
    return pos >= SHORT_STRIDE - n_tok


def _roll_conv(z, w):
    return w[2:3] * z + w[1:2] * pltpu.roll(z, 1, 0) + w[0:1] * pltpu.roll(z, 2, 0)


def _last_rows(z, n_seg):
    return [z[(s + 1) * SHORT_STRIDE - SUBLANES:(s + 1) * SHORT_STRIDE] for s in range(n_seg)]


def _ffn_short_kernel(x_ref, inj_ref, gn_ref, wup_ref, cw_ref, cb_ref, wd_ref, gfin_ref,
                      o_ref, tail_ref, *, final_norm, n_tok):
    n_seg = tail_ref.shape[0]
    f = wd_ref.shape[0]
    fc = FFN_COLS
    x = x_ref[...]
    h = _rmsnorm(x, gn_ref[...]).astype(_BF16)
    acc = x
    for j in range(f // fc):
        halves = []
        for c0 in (j * fc, f + j * fc):
            cols = slice(c0, c0 + fc)
            z = _dot(h, wup_ref[:, cols]) + inj_ref[:, cols]
            for s, rows in enumerate(_last_rows(z, n_seg)):
                tail_ref[s, :, cols] = rows
            halves.append(_roll_conv(z, cw_ref[:, cols]) + cb_ref[:, cols])
        a = (_silu(halves[0]) * halves[1]).astype(_BF16)
        acc = acc + _dot(a, wd_ref[j * fc:(j + 1) * fc, :])
    if final_norm:
        acc = _rmsnorm(acc, gfin_ref[...])
    o_ref[...] = jnp.where(_token_rows(x.shape[0], x.shape[1], n_tok), acc, 0.0)


def _ffn_short_call(x, inj, w, gfin, *, n_seg, n_tok, final_norm):
    rows, d = x.shape
    f2 = w["wup"].shape[1]
    args = (x, inj, w["gn"], w["wup"], w["cw"], w["cb"], w["wd"], gfin)
    return pl.pallas_call(
        functools.partial(_ffn_short_kernel, final_norm=final_norm, n_tok=n_tok),
        grid=(1,),
        in_specs=[_short_spec(a.shape) for a in args],
        out_specs=[_short_spec((rows, d)), _short_spec((n_seg, SUBLANES, f2))],
        out_shape=[jax.ShapeDtypeStruct((rows, d), _F32),
                   jax.ShapeDtypeStruct((n_seg, SUBLANES, f2), _F32)],
        compiler_params=_short_params(),
        name="conv_ffn_short",
    )(*args)


def _sc_short_kernel(x_ref, inj_ref, gn_ref, win_ref, cw_ref, wout_ref, o_ref, tail_ref, *, n_tok):
    n_seg = tail_ref.shape[0]
    x = x_ref[...]
    d = x.shape[1]
    h = _rmsnorm(x, gn_ref[...]).astype(_BF16)
    bg = _dot(h, win_ref[:, 0:d])
    ci = _dot(h, win_ref[:, d:2 * d]) * _dot(h, win_ref[:, 2 * d:3 * d]) + inj_ref[...]
    for s, rows in enumerate(_last_rows(ci, n_seg)):
        tail_ref[s] = rows
    y = _roll_conv(ci, cw_ref[...])
    out = x + _dot((bg * y).astype(_BF16), wout_ref[...])
    o_ref[...] = jnp.where(_token_rows(x.shape[0], d, n_tok), out, 0.0)


def _sc_short_call(x, inj, w, *, n_seg, n_tok):
    rows, d = x.shape
    args = (x, inj, w["gn"], w["win"], w["cw"], w["wout"])
    return pl.pallas_call(
        functools.partial(_sc_short_kernel, n_tok=n_tok),
        grid=(1,),
        in_specs=[_short_spec(a.shape) for a in args],
        out_specs=[_short_spec((rows, d)), _short_spec((n_seg, SUBLANES, d))],
        out_shape=[jax.ShapeDtypeStruct((rows, d), _F32),
                   jax.ShapeDtypeStruct((n_seg, SUBLANES, d), _F32)],
        compiler_params=_short_params(),
        name="shortconv_mixer_short",
    )(*args)


def _gla_short_kernel(x_ref, s0_ref, gn_ref, win_ref, wgd_ref, wgu_ref, bgate_ref, hg_ref, wout_ref,
                      o_ref, sout_ref, s_scr, p_scr, g_scr, o_scr, *, n_tok):
    n_seg = s0_ref.shape[0]
    c = SHORT_STRIDE
    hk = g_scr.shape[1]
    hv = o_scr.shape[1]
    x = x_ref[...]
    h = _rmsnorm(x, gn_ref[...]).astype(_BF16)
    p_scr[...] = _dot(h, win_ref[...])
    gdown = _dot(h, wgd_ref[...]).astype(_BF16)
    z = _dot(gdown, wgu_ref[...]) + bgate_ref[...]
    g = jnp.where(_token_rows(x.shape[0], hk, n_tok), _log_sigmoid(z) * (1.0 / GLA_TAU), 0.0)
    g_scr[...] = g
    sums = [jnp.sum(g[s * c:(s + 1) * c], axis=0, keepdims=True) for s in range(n_seg)]
    is_mild = jnp.min(functools.reduce(jnp.minimum, sums)) >= MILD_LOG_DECAY
    q_scale = float(hk // GLA_HEADS) ** -0.5
    masks = _level_masks(c)

    def walk(mild):
        def body(s, carry):
            rows = pl.ds(pl.multiple_of(s * c, c), c)
            q = p_scr[rows, 0:hk] * q_scale
            k = p_scr[rows, hk:2 * hk]
            v = p_scr[rows, 2 * hk:2 * hk + hv].astype(_BF16)
            s_scr[...] = s0_ref[s]
            pre = _gla_chunk_scores(q, k, v, g_scr[rows, :], c, masks, mild)
            o_scr[rows, :] = _gla_chunk_apply(pre, s_scr, hg_ref[...])
            sout_ref[s] = s_scr[...]
            return carry

        lax.fori_loop(0, n_seg, body, 0)

    @pl.when(is_mild)
    def _():
        walk(True)

    @pl.when(jnp.logical_not(is_mild))
    def _():
        walk(False)

    r = p_scr[:, 2 * hk + hv:2 * hk + 2 * hv]
    out = x + _dot((o_scr[...] * _silu(r)).astype(_BF16), wout_ref[...])
    o_ref[...] = jnp.where(_token_rows(x.shape[0], x.shape[1], n_tok), out, 0.0)


def _gla_short_call(x, s0, w, *, n_tok):
    rows, d = x.shape
    n_seg, n_h, dk, dv = s0.shape
    hk, hv = n_h * dk, n_h * dv
    args = (x, s0, w["gn"], w["win"], w["wgd"], w["wgu"], w["bgate"], w["hg"], w["wout"])
    return pl.pallas_call(
        functools.partial(_gla_short_kernel, n_tok=n_tok),
        grid=(1,),
        in_specs=[_short_spec(a.shape) for a in args],
        out_specs=[_short_spec((rows, d)), _short_spec(s0.shape)],
        out_shape=[jax.ShapeDtypeStruct((rows, d), _F32), jax.ShapeDtypeStruct(s0.shape, _F32)],
        scratch_shapes=[
            pltpu.VMEM((n_h, dk, dv), _F32),
            pltpu.VMEM((rows, 2 * hk + 2 * hv), _F32),
            pltpu.VMEM((rows, hk), _F32),
            pltpu.VMEM((rows, hv), _F32),
        ],
        compiler_params=_short_params(),
        name="gla_mixer_short",
    )(*args)


def _short_rows(a):
    n_seg, n_tok, width = a.shape
    return jnp.pad(a, ((0, 0), (SHORT_STRIDE - n_tok, 0), (0, 0))).reshape(n_seg * SHORT_STRIDE, width)


def _short_inject(cache, n_tok):
    n_seg, n_prev, width = cache.shape
    lead = SHORT_STRIDE - n_tok - n_prev
    return jnp.pad(cache, ((0, 0), (lead, n_tok), (0, 0))).reshape(n_seg * SHORT_STRIDE, width)


def _short_trunk(xs, gla_s, conv_c, ffn_c, params, gfin):
    n_seg, n_tok, d = xs.shape
    depth = len(params)
    x = _short_rows(xs)
    keep = lambda tail: tail[:, SUBLANES - (CONV_W - 1):, :]
    gla_new, conv_new, ffn_new = [], [], []
    for i, (mix, ffn) in enumerate(params):
        if i % 2 == 0:
            x, s = _gla_short_call(x, gla_s[i // 2], mix, n_tok=n_tok)
            gla_new.append(s)
        else:
            x, tail = _sc_short_call(x, _short_inject(conv_c[i // 2], n_tok), mix,
                                     n_seg=n_seg, n_tok=n_tok)
            conv_new.append(keep(tail))
        x, tail = _ffn_short_call(x, _short_inject(ffn_c[i], n_tok), ffn, gfin, n_seg=n_seg,
                                  n_tok=n_tok, final_norm=(i == depth - 1))
        ffn_new.append(keep(tail))
    y = x.reshape(n_seg, SHORT_STRIDE, d)[:, SHORT_STRIDE - n_tok:, :]
    return y, gla_new, conv_new, ffn_new


def _row(v):
    return v.reshape(1, -1).astype(_F32)


def _prep_gla(norm, w_in, w_gate_up, b_gate, head_gain, w_out):
    main = w_in.shape[1] - GLA_GATE_RANK
    wgd = jnp.pad(w_in[:, main:], ((0, 0), (0, LANES - GLA_GATE_RANK)))
    wgu = jnp.pad(w_gate_up, ((0, LANES - GLA_GATE_RANK), (0, 0)))
    return dict(gn=_row(norm), win=w_in[:, :main].astype(_BF16), wgd=wgd.astype(_BF16),
                wgu=wgu.astype(_BF16), bgate=_row(b_gate), hg=_row(head_gain),
                wout=w_out.astype(_BF16))


def _prep_sc(norm, w_in, conv_w, w_out):
    return dict(gn=_row(norm), win=w_in.astype(_BF16), cw=conv_w.astype(_F32),
                wout=w_out.astype(_BF16))


def _prep_ffn(norm, w_up, conv_w, conv_b, w_down):
    assert w_down.shape[0] % FFN_COLS == 0
    return dict(gn=_row(norm), wup=w_up.astype(_BF16), cw=conv_w.astype(_F32), cb=_row(conv_b),
                wd=w_down.astype(_BF16))


def _tail_from_cache(cache):
    return jnp.pad(cache, ((0, 0), (SUBLANES - (CONV_W - 1), 0), (0, 0)))


def _ffn_tail_from_cache(cache, n_c, fc):
    b = cache.shape[0]
    t = _tail_from_cache(cache).reshape(b, SUBLANES, 2, n_c, fc)
    return t.transpose(0, 2, 3, 1, 4)


def _ffn_cache_from_tail(tail):
    b = tail.shape[0]
    rows = tail[:, :, :, SUBLANES - (CONV_W - 1):, :]
    return rows.transpose(0, 3, 1, 2, 4).reshape(b, CONV_W - 1, -1)


def _trunk(x, gla_s, conv_tail, ffn_tail, params, gfin, *, tm_gla, tm_sc, tm_ffn, chunk):
    n_seq = x.shape[0]
    depth = len(params)
    gla_new, conv_new, ffn_new = [], [], []
    i_gla = i_conv = 0
    for i, (mix, ffn) in enumerate(params):
        if i % 2 == 0:
            x, s = _gla_call(x, gla_s[i_gla], mix, n_seq=n_seq, tm=tm_gla, chunk=chunk)
            gla_new.append(s)
            i_gla += 1
        else:
            x, s = _sc_call(x, conv_tail[i_conv], mix, n_seq=n_seq, tm=tm_sc)
            conv_new.append(s)
            i_conv += 1
        x, s = _ffn_call(x, ffn_tail[i], ffn, gfin, n_seq=n_seq, tm=tm_ffn,
                         final_norm=(i == depth - 1))
        ffn_new.append(s)
    return x, gla_new, conv_new, ffn_new


def kernel(x_prompt, x_sample, state_gla, cache_conv, cache_ffn, meta, norm_mix, norm_ffn, norm_final,
           gla_w_in, gla_w_gate_up, gla_b_gate, gla_head_gain, gla_w_out,
           sc_w_in, sc_conv_w, sc_w_out, ffn_w_up, ffn_conv_w, ffn_conv_b, ffn_w_down):
    depth = norm_mix.shape[0]
    n_b, seq, d = x_prompt.shape
    n_s, s_len, _ = x_sample.shape
    assert s_len == N_META and seq % GLA_CHUNK == 0
    f = ffn_w_down.shape[1]
    n_c, fc = f // FFN_COLS, FFN_COLS

    params = []
    for i in range(depth):
        j = i // 2
        if i % 2 == 0:
            mix = _prep_gla(norm_mix[i], gla_w_in[j], gla_w_gate_up[j], gla_b_gate[j],
                            gla_head_gain[j], gla_w_out[j])
        else:
            mix = _prep_sc(norm_mix[i], sc_w_in[j], sc_conv_w[j], sc_w_out[j])
        params.append((mix, _prep_ffn(norm_ffn[i], ffn_w_up[i], ffn_conv_w[i], ffn_conv_b[i],
                                      ffn_w_down[i])))
    gfin = _row(norm_final)

    xs = jnp.concatenate([x_sample, meta[None].astype(x_sample.dtype)], axis=0)
    zero1 = lambda a: jnp.zeros((a.shape[0], 1) + a.shape[2:], a.dtype)
    gla_s = jnp.concatenate([state_gla, zero1(state_gla)], axis=1).astype(_F32)
    conv_c = jnp.concatenate([cache_conv, zero1(cache_conv)], axis=1).astype(_F32)
    ffn_c = jnp.concatenate([cache_ffn, zero1(cache_ffn)], axis=1).astype(_F32)
    ys, gla_1, conv_1, ffn_1 = _short_trunk(xs, gla_s, conv_c, ffn_c, params, gfin)

    yp, gla_2, conv_2, ffn_2 = _trunk(
        x_prompt, [s[n_s:] for s in gla_1],
        [_tail_from_cache(s[n_s:]) for s in conv_1],
        [_ffn_tail_from_cache(s[n_s:], n_c, fc) for s in ffn_1],
        params, gfin, tm_gla=min(TILE_GLA, seq), tm_sc=min(TILE_SC, seq), tm_ffn=min(TILE_FFN, seq),
        chunk=GLA_CHUNK)

    dt = x_prompt.dtype
    tail2 = lambda s: s[:, SUBLANES - (CONV_W - 1):, :]
    return (yp.astype(dt), ys[:n_s].astype(dt),
            jnp.stack(gla_2).astype(dt), jnp.stack([s[:n_s] for s in gla_1]).astype(dt),
            jnp.stack([tail2(s) for s in conv_2]).astype(dt),
            jnp.stack([s[:n_s] for s in conv_1]).astype(dt),
            jnp.stack([_ffn_cache_from_tail(s) for s in ffn_2]).astype(dt),
            jnp.stack([s[:n_s] for s in ffn_1]).astype(dt))
```

```python
import functools

import jax
import jax.numpy as jnp
from jax import lax
from jax.experimental import pallas as pl
from jax.experimental.pallas import tpu as pltpu

N_META = 16
GLA_HEADS = 4
GLA_GATE_RANK = 16
GLA_TAU = 16.0
CONV_W = 3
EPS = 1e-6
GLA_CHUNK = 64
MILD_LOG_DECAY = -60.0
FFN_COLS = 256
TILE_GLA = 512
TILE_SC = 512
TILE_FFN = 512
LANES = 128
SUBLANES = 8
VMEM_LIMIT_BYTES = 56 * 1024 * 1024

_BF16 = jnp.bfloat16
_F32 = jnp.float32
_LOG2E = 1.4426950408889634


def _dot(a, b):
    return lax.dot_general(a, b, (((1,), (0,)), ((), ())), preferred_element_type=_F32)


def _dot_nt(a, b):
    return lax.dot_general(a, b, (((1,), (1,)), ((), ())), preferred_element_type=_F32)


def _dot_tn(a, b):
    return lax.dot_general(a, b, (((0,), (0,)), ((), ())), preferred_element_type=_F32)


def _rmsnorm(x, g):
    return x * lax.rsqrt(jnp.mean(x * x, axis=-1, keepdims=True) + EPS) * g


def _silu(x):
    return x / (1.0 + jnp.exp2(x * (-_LOG2E)))


def _log_sigmoid(z):
    return jnp.minimum(z, 0.0) - jnp.log(1.0 + jnp.exp(-jnp.abs(z)))


def _shift_rows(z, tail, k):
    rolled = pltpu.roll(z, k, 0)
    head = rolled[:SUBLANES]
    rid = lax.broadcasted_iota(jnp.int32, head.shape, 0)
    head = jnp.where(rid < k, pltpu.roll(tail, k, 0), head)
    if z.shape[0] == SUBLANES:
        return head
    return jnp.concatenate([head, rolled[SUBLANES:]], axis=0)


def _causal_conv(z, tail, w):
    return w[2:3] * z + w[1:2] * _shift_rows(z, tail, 1) + w[0:1] * _shift_rows(z, tail, 2)


def _const_spec(shape):
    nd = len(shape)
    return pl.BlockSpec(shape, lambda b, t: (0,) * nd, pipeline_mode=pl.Buffered(1))


def _init_spec(shape, shared):
    nd = len(shape)
    block = (1,) + tuple(shape[1:])
    if shared:
        return pl.BlockSpec(block, lambda b, t: (0,) * nd)
    return pl.BlockSpec(block, lambda b, t: (b,) + (0,) * (nd - 1))


def _tile_spec(tm, d):
    return pl.BlockSpec((1, tm, d), lambda b, t: (b, t, 0))


def _next_tile_spec(tm, d, n_t):
    return pl.BlockSpec((1, tm, d), lambda b, t: (b, jnp.minimum(t + 1, n_t - 1), 0))


def _compiler_params():
    return pltpu.CompilerParams(dimension_semantics=("arbitrary", "arbitrary"),
                                vmem_limit_bytes=VMEM_LIMIT_BYTES)


def _ffn_kernel(x_ref, init_ref, gn_ref, wup_ref, cw_ref, cb_ref, wd_ref,
                gfin_ref, o_ref, cache_ref, zg_buf, zu_buf, *, final_norm):
    t = pl.program_id(1)
    n_t = pl.num_programs(1)
    n_c, _, fc = zg_buf.shape
    f = n_c * fc
    tm = x_ref.shape[1]
    s8 = SUBLANES

    @pl.when(t == 0)
    def _():
        zg_buf[:, 0:s8, :] = init_ref[0, 0]
        zu_buf[:, 0:s8, :] = init_ref[0, 1]

    x = x_ref[0]
    h = _rmsnorm(x, gn_ref[...]).astype(_BF16)

    def up(j):
        zg_buf[j, s8:s8 + tm, :] = _dot(h, wup_ref[:, j * fc:(j + 1) * fc])
        zu_buf[j, s8:s8 + tm, :] = _dot(h, wup_ref[:, f + j * fc:f + (j + 1) * fc])

    def conv(buf, j, c0):
        w = cw_ref[:, c0:c0 + fc]
        return (w[2:3] * buf[j, s8:s8 + tm, :] + w[1:2] * buf[j, s8 - 1:s8 - 1 + tm, :]
                + w[0:1] * buf[j, s8 - 2:s8 - 2 + tm, :] + cb_ref[:, c0:c0 + fc])

    def act(j):
        a = _silu(conv(zg_buf, j, j * fc)) * conv(zu_buf, j, f + j * fc)
        zg_buf[j, 0:s8, :] = zg_buf[j, tm:tm + s8, :]
        zu_buf[j, 0:s8, :] = zu_buf[j, tm:tm + s8, :]
        return a.astype(_BF16)

    for j in range(n_c):
        up(j)
    acts = [act(j) for j in range(n_c)]
    acc = x
    for j in range(n_c):
        acc = acc + _dot(acts[j], wd_ref[j * fc:(j + 1) * fc, :])
    if final_norm:
        acc = _rmsnorm(acc, gfin_ref[...])
    o_ref[0] = acc

    @pl.when(t == n_t - 1)
    def _():
        cache_ref[0, 0] = zg_buf[:, 0:s8, :]
        cache_ref[0, 1] = zu_buf[:, 0:s8, :]


def _ffn_call(x, init, w, gfin, *, n_seq, tm, final_norm):
    _, seq, d = x.shape
    fc = FFN_COLS
    n_c = w["wd"].shape[0] // fc
    shared = init.shape[0] == 1 and n_seq > 1
    tail_shape = (n_c, SUBLANES, fc)
    out, cache = pl.pallas_call(
        functools.partial(_ffn_kernel, final_norm=final_norm),
        grid=(n_seq, seq // tm),
        in_specs=[
            _tile_spec(tm, d),
            _init_spec(init.shape, shared),
            _const_spec(w["gn"].shape),
            _const_spec(w["wup"].shape), _const_spec(w["cw"].shape), _const_spec(w["cb"].shape),
            _const_spec(w["wd"].shape),
            _const_spec(gfin.shape),
        ],
        out_specs=[
            _tile_spec(tm, d),
            pl.BlockSpec((1, 2) + tail_shape, lambda b, t: (b, 0, 0, 0, 0)),
        ],
        out_shape=[
            jax.ShapeDtypeStruct((n_seq, seq, d), _F32),
            jax.ShapeDtypeStruct((n_seq, 2) + tail_shape, _F32),
        ],
        scratch_shapes=[pltpu.VMEM((n_c, SUBLANES + tm, fc), _F32),
                        pltpu.VMEM((n_c, SUBLANES + tm, fc), _F32)],
        compiler_params=_compiler_params(),
        name="conv_ffn",
    )(x, init, w["gn"], w["wup"], w["cw"], w["cb"], w["wd"], gfin)
    return out, cache


def _sc_kernel(x_ref, xn_ref, init_ref, gn_ref, win_ref, cw_ref, wout_ref, o_ref, cache_ref,
               tail, p_scr, *, pipelined):
    t = pl.program_id(1)
    n_t = pl.num_programs(1)
    tm = x_ref.shape[1]

    d = x_ref.shape[2]

    def project(h, slot):
        for i in range(3):
            p_scr[slot, i] = _dot(h, win_ref[:, i * d:(i + 1) * d])

    @pl.when(t == 0)
    def _():
        tail[...] = init_ref[0]
        project(_rmsnorm(x_ref[0], gn_ref[...]).astype(_BF16), 0)

    if pipelined:
        cur = lax.rem(t, 2)
        project(_rmsnorm(xn_ref[0], gn_ref[...]).astype(_BF16), 1 - cur)
    else:
        cur = 0
    ci = p_scr[cur, 1] * p_scr[cur, 2]
    y = _causal_conv(ci, tail[...], cw_ref[...])
    tail[...] = ci[tm - SUBLANES:]
    o_ref[0] = x_ref[0] + _dot((p_scr[cur, 0] * y).astype(_BF16), wout_ref[...])

    @pl.when(t == n_t - 1)
    def _():
        cache_ref[0] = tail[...]


def _sc_call(x, init, w, *, n_seq, tm):
    _, seq, d = x.shape
    n_t = seq // tm
    shared = init.shape[0] == 1 and n_seq > 1
    out, cache = pl.pallas_call(
        functools.partial(_sc_kernel, pipelined=n_t > 1),
        grid=(n_seq, n_t),
        in_specs=[
            _tile_spec(tm, d), _next_tile_spec(tm, d, n_t),
            _init_spec(init.shape, shared),
            _const_spec(w["gn"].shape), _const_spec(w["win"].shape),
            _const_spec(w["cw"].shape), _const_spec(w["wout"].shape),
        ],
        out_specs=[
            _tile_spec(tm, d),
            pl.BlockSpec((1, SUBLANES, d), lambda b, t: (b, 0, 0)),
        ],
        out_shape=[
            jax.ShapeDtypeStruct((n_seq, seq, d), _F32),
            jax.ShapeDtypeStruct((n_seq, SUBLANES, d), _F32),
        ],
        scratch_shapes=[pltpu.VMEM((SUBLANES, d), _F32), pltpu.VMEM((2, 3, tm, d), _F32)],
        compiler_params=_compiler_params(),
        name="shortconv_mixer",
    )(x, x, init, w["gn"], w["win"], w["cw"], w["wout"])
    return out, cache


def _cumsum_rows(a):
    n = a.shape[0]
    rid = lax.broadcasted_iota(jnp.int32, (SUBLANES, a.shape[1]), 0)
    groups = []
    for i in range(0, n, SUBLANES):
        grp = a[i:i + SUBLANES]
        s = 1
        while s < SUBLANES:
            grp = grp + jnp.where(rid >= s, pltpu.roll(grp, s, 0), 0.0)
            s *= 2
        if groups:
            grp = grp + groups[-1][SUBLANES - 1:SUBLANES]
        groups.append(grp)
    return jnp.concatenate(groups, axis=0) if len(groups) > 1 else groups[0]


def _level_masks(c):
    tt = lax.broadcasted_iota(jnp.int32, (c, c), 0)
    ss = lax.broadcasted_iota(jnp.int32, (c, c), 1)
    txs = tt ^ ss
    lower = tt > ss
    masks = [tt == ss]
    half = 1
    while half < c:
        masks.append(lower & (txs >= half) & (txs < 2 * half))
        half *= 2
    return masks


def _gla_chunk_scores(q, k, g, c, masks, mild):
    n_h = GLA_HEADS
    dk = q.shape[1] // n_h
    b = _cumsum_rows(g)
    b_last = b[c - 1:c]
    qe = (q * jnp.exp(b)).astype(_BF16)
    kd = (k * jnp.exp(b_last - b)).astype(_BF16)
    decay = jnp.exp(b_last)

    if mild:
        k_inv = (k * jnp.exp(-b)).astype(_BF16)
        tt = lax.broadcasted_iota(jnp.int32, (c, c), 0)
        causal = tt >= lax.broadcasted_iota(jnp.int32, (c, c), 1)
        scores = []
        for hh in range(n_h):
            ks = slice(hh * dk, (hh + 1) * dk)
            a = jnp.where(causal, _dot_nt(qe[:, ks], k_inv[:, ks]), 0.0)
            scores.append(a.astype(_BF16))
        return dict(scores=scores, qe=qe, kd=kd, decay=decay)

    rid = lax.broadcasted_iota(jnp.int32, b.shape, 0)
    levels = [(q.astype(_BF16), k.astype(_BF16))]
    e_k = b
    half = 1
    while half < c:
        hi = (rid & half) != 0
        b_mid = jnp.where(hi, pltpu.roll(e_k, half, 0), e_k)
        scale = jnp.exp2((b - b_mid) * jnp.where(hi, _LOG2E, -_LOG2E))
        x = (jnp.where(hi, q, k) * scale).astype(_BF16)
        levels.append((x, x))
        if 2 * half < c:
            e_k = jnp.where(hi, e_k, pltpu.roll(e_k, c - half, 0))
        half *= 2

    scores = []
    for hh in range(n_h):
        ks = slice(hh * dk, (hh + 1) * dk)
        a = jnp.zeros((c, c), _F32)
        for mask, (qs, kk) in zip(masks, levels):
            a = jnp.where(mask, _dot_nt(qs[:, ks], kk[:, ks]), a)
        scores.append(a.astype(_BF16))
    return dict(scores=scores, qe=qe, kd=kd, decay=decay)


def _gla_chunk_apply(pre, v, s_ref, hg):
    n_h = GLA_HEADS
    qe, kd, decay = pre["qe"], pre["kd"], pre["decay"]
    dk = qe.shape[1] // n_h
    dv = v.shape[1] // n_h
    outs = []
    for hh in range(n_h):
        ks = slice(hh * dk, (hh + 1) * dk)
        vs = slice(hh * dv, (hh + 1) * dv)
        s_old = s_ref[hh]
        o = _dot(qe[:, ks], s_old.astype(_BF16)) + _dot(pre["scores"][hh], v[:, vs])
        o = o * lax.rsqrt(jnp.mean(o * o, axis=-1, keepdims=True) + EPS) * hg[:, vs]
        outs.append(o)
        dcol = jnp.transpose(jnp.broadcast_to(decay[:, ks], (dk, dk)))
        dmat = jnp.concatenate([dcol] * (dv // dk), axis=1)
        s_ref[hh] = dmat * s_old + _dot_tn(kd[:, ks], v[:, vs])
    return jnp.concatenate(outs, axis=1)


def _gla_kernel(x_ref, s0_ref, gn_ref, win_ref, wgd_ref, wgu_ref, bgate_ref, hg_ref, wout_ref,
                o_ref, sout_ref, s_scr, p_scr, g_scr, o_scr, h_scr, *, chunk):
    t = pl.program_id(1)
    n_t = pl.num_programs(1)
    tm = x_ref.shape[1]
    hk = g_scr.shape[1]
    hv = o_scr.shape[1]
    n_chunks = tm // chunk

    @pl.when(t == 0)
    def _():
        s_scr[...] = s0_ref[0]

    h_scr[...] = _rmsnorm(x_ref[0], gn_ref[...]).astype(_BF16)
    h = h_scr[...]
    p_scr[:, 0:2 * hk] = _dot(h, win_ref[:, 0:2 * hk])
    gdown = _dot(h, wgd_ref[...]).astype(_BF16)
    z = _dot(gdown, wgu_ref[...]) + bgate_ref[...]
    g = _log_sigmoid(z) * (1.0 / GLA_TAU)
    g_scr[...] = g
    chunk_sums = [jnp.sum(g[ci * chunk:(ci + 1) * chunk], axis=0, keepdims=True)
                  for ci in range(n_chunks)]
    lowest = jnp.min(functools.reduce(jnp.minimum, chunk_sums))
    q_scale = float(hk // GLA_HEADS) ** -0.5

    def step_body(mild):
        masks = _level_masks(chunk)
        hb = h_scr[...]
        half = hv // 2
        pieces = [(2 * hk + i * half, 2 * hk + (i + 1) * half) for i in range(4)]

        def project(i):
            c0, c1 = pieces[i]
            p_scr[:, c0:c1] = _dot(hb, win_ref[:, c0:c1])

        def scores(ci):
            rows = slice(ci * chunk, (ci + 1) * chunk)
            q = p_scr[rows, 0:hk] * q_scale
            k = p_scr[rows, hk:2 * hk]
            return _gla_chunk_scores(q, k, g_scr[rows, :], chunk, masks, mild)

        def apply(ci, pre):
            rows = slice(ci * chunk, (ci + 1) * chunk)
            v = p_scr[rows, 2 * hk:2 * hk + hv].astype(_BF16)
            o_scr[rows, :] = _gla_chunk_apply(pre, v, s_scr, hg_ref[...])

        project(0)
        pre = scores(0)
        project(1)
        for ci in range(n_chunks):
            nxt_pre = scores(ci + 1) if ci + 1 < n_chunks else None
            apply(ci, pre)
            if ci < 2:
                project(2 + ci)
            pre = nxt_pre
        for i in range(2 + min(2, n_chunks), 4):
            project(i)
        r = p_scr[:, 2 * hk + hv:2 * hk + 2 * hv]
        o_ref[0] = x_ref[0] + _dot((o_scr[...] * _silu(r)).astype(_BF16), wout_ref[...])

    is_mild = lowest >= MILD_LOG_DECAY

    @pl.when(is_mild)
    def _():
        step_body(True)

    @pl.when(jnp.logical_not(is_mild))
    def _():
        step_body(False)

    @pl.when(t == n_t - 1)
    def _():
        sout_ref[0] = s_scr[...]


def _gla_call(x, s0, w, *, n_seq, tm, chunk):
    _, seq, d = x.shape
    n_h, dk, dv = s0.shape[1:]
    hk, hv = n_h * dk, n_h * dv
    shared = s0.shape[0] == 1 and n_seq > 1
    out, s_out = pl.pallas_call(
        functools.partial(_gla_kernel, chunk=chunk),
        grid=(n_seq, seq // tm),
        in_specs=[
            _tile_spec(tm, d),
            _init_spec(s0.shape, shared),
            _const_spec(w["gn"].shape), _const_spec(w["win"].shape),
            _const_spec(w["wgd"].shape), _const_spec(w["wgu"].shape),
            _const_spec(w["bgate"].shape), _const_spec(w["hg"].shape),
            _const_spec(w["wout"].shape),
        ],
        out_specs=[
            _tile_spec(tm, d),
            pl.BlockSpec((1, n_h, dk, dv), lambda b, t: (b, 0, 0, 0)),
        ],
        out_shape=[
            jax.ShapeDtypeStruct((n_seq, seq, d), _F32),
            jax.ShapeDtypeStruct((n_seq, n_h, dk, dv), _F32),
        ],
        scratch_shapes=[
            pltpu.VMEM((n_h, dk, dv), _F32),
            pltpu.VMEM((tm, 2 * hk + 2 * hv), _F32),
            pltpu.VMEM((tm, hk), _F32),
            pltpu.VMEM((tm, hv), _F32),
            pltpu.VMEM((tm, d), _BF16),
        ],
        compiler_params=_compiler_params(),
        name="gla_mixer",
    )(x, s0, w["gn"], w["win"], w["wgd"], w["wgu"], w["bgate"], w["hg"], w["wout"])
    return out, s_out


SHORT_STRIDE = 32


def _short_spec(shape):
    nd = len(shape)
    return pl.BlockSpec(shape, lambda i: (0,) * nd, pipeline_mode=pl.Buffered(1))


def _short_params():
    return pltpu.CompilerParams(dimension_semantics=("arbitrary",), vmem_limit_bytes=VMEM_LIMIT_BYTES)


def _token_rows(n_rows, width, n_tok):
    pos = lax.broadcasted_iota(jnp.int32, (n_rows, width), 0) & (SHORT_STRIDE - 1)
    return pos >= SHORT_STRIDE - n_tok


def _roll_conv(z, w):
    return w[2:3] * z + w[1:2] * pltpu.roll(z, 1, 0) + w[0:1] * pltpu.roll(z, 2, 0)


def _last_rows(z, n_seg):
    return [z[(s + 1) * SHORT_STRIDE - SUBLANES:(s + 1) * SHORT_STRIDE] for s in range(n_seg)]


def _ffn_short_kernel(x_ref, inj_ref, gn_ref, wup_ref, cw_ref, cb_ref, wd_ref, gfin_ref,
                      o_ref, tail_ref, *, final_norm, n_tok):
    n_seg = tail_ref.shape[0]
    f = wd_ref.shape[0]
    fc = FFN_COLS
    x = x_ref[...]
    h = _rmsnorm(x, gn_ref[...]).astype(_BF16)
    acc = x
    for j in range(f // fc):
        halves = []
        for c0 in (j * fc, f + j * fc):
            cols = slice(c0, c0 + fc)
            z = _dot(h, wup_ref[:, cols]) + inj_ref[:, cols]
            for s, rows in enumerate(_last_rows(z, n_seg)):
                tail_ref[s, :, cols] = rows
            halves.append(_roll_conv(z, cw_ref[:, cols]) + cb_ref[:, cols])
        a = (_silu(halves[0]) * halves[1]).astype(_BF16)
        acc = acc + _dot(a, wd_ref[j * fc:(j + 1) * fc, :])
    if final_norm:
        acc = _rmsnorm(acc, gfin_ref[...])
    o_ref[...] = jnp.where(_token_rows(x.shape[0], x.shape[1], n_tok), acc, 0.0)


def _ffn_short_call(x, inj, w, gfin, *, n_seg, n_tok, final_norm):
    rows, d = x.shape
    f2 = w["wup"].shape[1]
    args = (x, inj, w["gn"], w["wup"], w["cw"], w["cb"], w["wd"], gfin)
    return pl.pallas_call(
        functools.partial(_ffn_short_kernel, final_norm=final_norm, n_tok=n_tok),
        grid=(1,),
        in_specs=[_short_spec(a.shape) for a in args],
        out_specs=[_short_spec((rows, d)), _short_spec((n_seg, SUBLANES, f2))],
        out_shape=[jax.ShapeDtypeStruct((rows, d), _F32),
                   jax.ShapeDtypeStruct((n_seg, SUBLANES, f2), _F32)],
        compiler_params=_short_params(),
        name="conv_ffn_short",
    )(*args)


def _sc_short_kernel(x_ref, inj_ref, gn_ref, win_ref, cw_ref, wout_ref, o_ref, tail_ref, *, n_tok):
    n_seg = tail_ref.shape[0]
    x = x_ref[...]
    d = x.shape[1]
    h = _rmsnorm(x, gn_ref[...]).astype(_BF16)
    bg = _dot(h, win_ref[:, 0:d])
    ci = _dot(h, win_ref[:, d:2 * d]) * _dot(h, win_ref[:, 2 * d:3 * d]) + inj_ref[...]
    for s, rows in enumerate(_last_rows(ci, n_seg)):
        tail_ref[s] = rows
    y = _roll_conv(ci, cw_ref[...])
    out = x + _dot((bg * y).astype(_BF16), wout_ref[...])
    o_ref[...] = jnp.where(_token_rows(x.shape[0], d, n_tok), out, 0.0)


def _sc_short_call(x, inj, w, *, n_seg, n_tok):
    rows, d = x.shape
    args = (x, inj, w["gn"], w["win"], w["cw"], w["wout"])
    return pl.pallas_call(
        functools.partial(_sc_short_kernel, n_tok=n_tok),
        grid=(1,),
        in_specs=[_short_spec(a.shape) for a in args],
        out_specs=[_short_spec((rows, d)), _short_spec((n_seg, SUBLANES, d))],
        out_shape=[jax.ShapeDtypeStruct((rows, d), _F32),
                   jax.ShapeDtypeStruct((n_seg, SUBLANES, d), _F32)],
        compiler_params=_short_params(),
        name="shortconv_mixer_short",
    )(*args)


def _gla_short_kernel(x_ref, s0_ref, gn_ref, win_ref, wgd_ref, wgu_ref, bgate_ref, hg_ref, wout_ref,
                      o_ref, sout_ref, s_scr, p_scr, g_scr, o_scr, *, n_tok):
    n_seg = s0_ref.shape[0]
    c = SHORT_STRIDE
    hk = g_scr.shape[1]
    hv = o_scr.shape[1]
    x = x_ref[...]
    h = _rmsnorm(x, gn_ref[...]).astype(_BF16)
    p_scr[...] = _dot(h, win_ref[...])
    gdown = _dot(h, wgd_ref[...]).astype(_BF16)
    z = _dot(gdown, wgu_ref[...]) + bgate_ref[...]
    g = jnp.where(_token_rows(x.shape[0], hk, n_tok), _log_sigmoid(z) * (1.0 / GLA_TAU), 0.0)
    g_scr[...] = g
    sums = [jnp.sum(g[s * c:(s + 1) * c], axis=0, keepdims=True) for s in range(n_seg)]
    is_mild = jnp.min(functools.reduce(jnp.minimum, sums)) >= MILD_LOG_DECAY
    q_scale = float(hk // GLA_HEADS) ** -0.5
    masks = _level_masks(c)

    def walk(mild):
        def body(s, carry):
            rows = pl.ds(pl.multiple_of(s * c, c), c)
            q = p_scr[rows, 0:hk] * q_scale
            k = p_scr[rows, hk:2 * hk]
            v = p_scr[rows, 2 * hk:2 * hk + hv].astype(_BF16)
            s_scr[...] = s0_ref[s]
            pre = _gla_chunk_scores(q, k, g_scr[rows, :], c, masks, mild)
            o_scr[rows, :] = _gla_chunk_apply(pre, v, s_scr, hg_ref[...])
            sout_ref[s] = s_scr[...]
            return carry

        lax.fori_loop(0, n_seg, body, 0)

    @pl.when(is_mild)
    def _():
        walk(True)

    @pl.when(jnp.logical_not(is_mild))
    def _():
        walk(False)

    r = p_scr[:, 2 * hk + hv:2 * hk + 2 * hv]
    out = x + _dot((o_scr[...] * _silu(r)).astype(_BF16), wout_ref[...])
    o_ref[...] = jnp.where(_token_rows(x.shape[0], x.shape[1], n_tok), out, 0.0)


def _gla_short_call(x, s0, w, *, n_tok):
    rows, d = x.shape
    n_seg, n_h, dk, dv = s0.shape
    hk, hv = n_h * dk, n_h * dv
    args = (x, s0, w["gn"], w["win"], w["wgd"], w["wgu"], w["bgate"], w["hg"], w["wout"])
    return pl.pallas_call(
        functools.partial(_gla_short_kernel, n_tok=n_tok),
        grid=(1,),
        in_specs=[_short_spec(a.shape) for a in args],
        out_specs=[_short_spec((rows, d)), _short_spec(s0.shape)],
        out_shape=[jax.ShapeDtypeStruct((rows, d), _F32), jax.ShapeDtypeStruct(s0.shape, _F32)],
        scratch_shapes=[
            pltpu.VMEM((n_h, dk, dv), _F32),
            pltpu.VMEM((rows, 2 * hk + 2 * hv), _F32),
            pltpu.VMEM((rows, hk), _F32),
            pltpu.VMEM((rows, hv), _F32),
        ],
        compiler_params=_short_params(),
        name="gla_mixer_short",
    )(*args)


def _short_rows(a):
    n_seg, n_tok, width = a.shape
    return jnp.pad(a, ((0, 0), (SHORT_STRIDE - n_tok, 0), (0, 0))).reshape(n_seg * SHORT_STRIDE, width)


def _short_inject(cache, n_tok):
    n_seg, n_prev, width = cache.shape
    lead = SHORT_STRIDE - n_tok - n_prev
    return jnp.pad(cache, ((0, 0), (lead, n_tok), (0, 0))).reshape(n_seg * SHORT_STRIDE, width)


def _short_trunk(xs, gla_s, conv_c, ffn_c, params, gfin):
    n_seg, n_tok, d = xs.shape
    depth = len(params)
    x = _short_rows(xs)
    keep = lambda tail: tail[:, SUBLANES - (CONV_W - 1):, :]
    gla_new, conv_new, ffn_new = [], [], []
    for i, (mix, ffn) in enumerate(params):
        if i % 2 == 0:
            x, s = _gla_short_call(x, gla_s[i // 2], mix, n_tok=n_tok)
            gla_new.append(s)
        else:
            x, tail = _sc_short_call(x, _short_inject(conv_c[i // 2], n_tok), mix,
                                     n_seg=n_seg, n_tok=n_tok)
            conv_new.append(keep(tail))
        x, tail = _ffn_short_call(x, _short_inject(ffn_c[i], n_tok), ffn, gfin, n_seg=n_seg,
                                  n_tok=n_tok, final_norm=(i == depth - 1))
        ffn_new.append(keep(tail))
    y = x.reshape(n_seg, SHORT_STRIDE, d)[:, SHORT_STRIDE - n_tok:, :]
    return y, gla_new, conv_new, ffn_new


def _row(v):
    return v.reshape(1, -1).astype(_F32)


def _prep_gla(norm, w_in, w_gate_up, b_gate, head_gain, w_out):
    main = w_in.shape[1] - GLA_GATE_RANK
    wgd = jnp.pad(w_in[:, main:], ((0, 0), (0, LANES - GLA_GATE_RANK)))
    wgu = jnp.pad(w_gate_up, ((0, LANES - GLA_GATE_RANK), (0, 0)))
    return dict(gn=_row(norm), win=w_in[:, :main].astype(_BF16), wgd=wgd.astype(_BF16),
                wgu=wgu.astype(_BF16), bgate=_row(b_gate), hg=_row(head_gain),
                wout=w_out.astype(_BF16))


def _prep_sc(norm, w_in, conv_w, w_out):
    return dict(gn=_row(norm), win=w_in.astype(_BF16), cw=conv_w.astype(_F32),
                wout=w_out.astype(_BF16))


def _prep_ffn(norm, w_up, conv_w, conv_b, w_down):
    assert w_down.shape[0] % FFN_COLS == 0
    return dict(gn=_row(norm), wup=w_up.astype(_BF16), cw=conv_w.astype(_F32), cb=_row(conv_b),
                wd=w_down.astype(_BF16))


def _tail_from_cache(cache):
    return jnp.pad(cache, ((0, 0), (SUBLANES - (CONV_W - 1), 0), (0, 0)))


def _ffn_tail_from_cache(cache, n_c, fc):
    b = cache.shape[0]
    t = _tail_from_cache(cache).reshape(b, SUBLANES, 2, n_c, fc)
    return t.transpose(0, 2, 3, 1, 4)


def _ffn_cache_from_tail(tail):
    b = tail.shape[0]
    rows = tail[:, :, :, SUBLANES - (CONV_W - 1):, :]
    return rows.transpose(0, 3, 1, 2, 4).reshape(b, CONV_W - 1, -1)


def _trunk(x, gla_s, conv_tail, ffn_tail, params, gfin, *, tm_gla, tm_sc, tm_ffn, chunk):
    n_seq = x.shape[0]
    depth = len(params)
    gla_new, conv_new, ffn_new = [], [], []
    i_gla = i_conv = 0
    for i, (mix, ffn) in enumerate(params):
        if i % 2 == 0:
            x, s = _gla_call(x, gla_s[i_gla], mix, n_seq=n_seq, tm=tm_gla, chunk=chunk)
            gla_new.append(s)
            i_gla += 1
        else:
            x, s = _sc_call(x, conv_tail[i_conv], mix, n_seq=n_seq, tm=tm_sc)
            conv_new.append(s)
            i_conv += 1
        x, s = _ffn_call(x, ffn_tail[i], ffn, gfin, n_seq=n_seq, tm=tm_ffn,
                         final_norm=(i == depth - 1))
        ffn_new.append(s)
    return x, gla_new, conv_new, ffn_new


def kernel(x_prompt, x_sample, state_gla, cache_conv, cache_ffn, meta, norm_mix, norm_ffn, norm_final,
           gla_w_in, gla_w_gate_up, gla_b_gate, gla_head_gain, gla_w_out,
           sc_w_in, sc_conv_w, sc_w_out, ffn_w_up, ffn_conv_w, ffn_conv_b, ffn_w_down):
    depth = norm_mix.shape[0]
    n_b, seq, d = x_prompt.shape
    n_s, s_len, _ = x_sample.shape
    assert s_len == N_META and seq % GLA_CHUNK == 0
    f = ffn_w_down.shape[1]
    n_c, fc = f // FFN_COLS, FFN_COLS

    params = []
    for i in range(depth):
        j = i // 2
        if i % 2 == 0:
            mix = _prep_gla(norm_mix[i], gla_w_in[j], gla_w_gate_up[j], gla_b_gate[j],
                            gla_head_gain[j], gla_w_out[j])
        else:
            mix = _prep_sc(norm_mix[i], sc_w_in[j], sc_conv_w[j], sc_w_out[j])
        params.append((mix, _prep_ffn(norm_ffn[i], ffn_w_up[i], ffn_conv_w[i], ffn_conv_b[i],
                                      ffn_w_down[i])))
    gfin = _row(norm_final)

    xs = jnp.concatenate([x_sample, meta[None].astype(x_sample.dtype)], axis=0)
    zero1 = lambda a: jnp.zeros((a.shape[0], 1) + a.shape[2:], a.dtype)
    gla_s = jnp.concatenate([state_gla, zero1(state_gla)], axis=1).astype(_F32)
    conv_c = jnp.concatenate([cache_conv, zero1(cache_conv)], axis=1).astype(_F32)
    ffn_c = jnp.concatenate([cache_ffn, zero1(cache_ffn)], axis=1).astype(_F32)
    ys, gla_1, conv_1, ffn_1 = _short_trunk(xs, gla_s, conv_c, ffn_c, params, gfin)

    yp, gla_2, conv_2, ffn_2 = _trunk(
        x_prompt, [s[n_s:] for s in gla_1],
        [_tail_from_cache(s[n_s:]) for s in conv_1],
        [_ffn_tail_from_cache(s[n_s:], n_c, fc) for s in ffn_1],
        params, gfin, tm_gla=min(TILE_GLA, seq), tm_sc=min(TILE_SC, seq), tm_ffn=min(TILE_FFN, seq),
        chunk=GLA_CHUNK)

    dt = x_prompt.dtype
    tail2 = lambda s: s[:, SUBLANES - (CONV_W - 1):, :]
    return (yp.astype(dt), ys[:n_s].astype(dt),
            jnp.stack(gla_2).astype(dt), jnp.stack([s[:n_s] for s in gla_1]).astype(dt),
            jnp.stack([tail2(s) for s in conv_2]).astype(dt),
            jnp.stack([s[:n_s] for s in conv_1]).astype(dt),
            jnp.stack([_ffn_cache_from_tail(s) for s in ffn_2]).astype(dt),
            jnp.stack([s[:n_s] for s in ffn_1]).astype(dt))
```

```python
import functools

import jax
import jax.numpy as jnp
from jax import lax
from jax.experimental import pallas as pl
from jax.experimental.pallas import tpu as pltpu

N_META = 16
GLA_HEADS = 4
GLA_GATE_RANK = 16
GLA_TAU = 16.0
CONV_W = 3
EPS = 1e-6
GLA_CHUNK = 64
MILD_LOG_DECAY = -60.0
FFN_COLS = 256
TILE_GLA = 512
TILE_SC = 512
TILE_FFN = 512
LANES = 128
SUBLANES = 8
VMEM_LIMIT_BYTES = 56 * 1024 * 1024

_BF16 = jnp.bfloat16
_F32 = jnp.float32
_LOG2E = 1.4426950408889634


def _dot(a, b):
    return lax.dot_general(a, b, (((1,), (0,)), ((), ())), preferred_element_type=_F32)


def _dot_nt(a, b):
    return lax.dot_general(a, b, (((1,), (1,)), ((), ())), preferred_element_type=_F32)


def _dot_tn(a, b):
    return lax.dot_general(a, b, (((0,), (0,)), ((), ())), preferred_element_type=_F32)


def _rmsnorm(x, g):
    return x * lax.rsqrt(jnp.mean(x * x, axis=-1, keepdims=True) + EPS) * g


def _silu(x):
    return x / (1.0 + jnp.exp2(x * (-_LOG2E)))


def _log_sigmoid(z):
    return jnp.minimum(z, 0.0) - jnp.log(1.0 + jnp.exp(-jnp.abs(z)))


def _shift_rows(z, tail, k):
    rolled = pltpu.roll(z, k, 0)
    head = rolled[:SUBLANES]
    rid = lax.broadcasted_iota(jnp.int32, head.shape, 0)
    head = jnp.where(rid < k, pltpu.roll(tail, k, 0), head)
    if z.shape[0] == SUBLANES:
        return head
    return jnp.concatenate([head, rolled[SUBLANES:]], axis=0)


def _causal_conv(z, tail, w):
    return w[2:3] * z + w[1:2] * _shift_rows(z, tail, 1) + w[0:1] * _shift_rows(z, tail, 2)


def _const_spec(shape):
    nd = len(shape)
    return pl.BlockSpec(shape, lambda b, t: (0,) * nd, pipeline_mode=pl.Buffered(1))


def _init_spec(shape, shared):
    nd = len(shape)
    block = (1,) + tuple(shape[1:])
    if shared:
        return pl.BlockSpec(block, lambda b, t: (0,) * nd)
    return pl.BlockSpec(block, lambda b, t: (b,) + (0,) * (nd - 1))


def _tile_spec(tm, d):
    return pl.BlockSpec((1, tm, d), lambda b, t: (b, t, 0))


def _next_tile_spec(tm, d, n_t):
    return pl.BlockSpec((1, tm, d), lambda b, t: (b, jnp.minimum(t + 1, n_t - 1), 0))


def _compiler_params(flags=None):
    return pltpu.CompilerParams(dimension_semantics=("arbitrary", "arbitrary"),
                                vmem_limit_bytes=VMEM_LIMIT_BYTES, flags=flags)


def _ffn_kernel(x_ref, init_ref, gn_ref, wup_ref, cw_ref, cb_ref, wd_ref,
                gfin_ref, o_ref, cache_ref, *z_bufs, final_norm):
    t = pl.program_id(1)
    n_t = pl.num_programs(1)
    n_c = len(z_bufs) // 2
    zg_bufs, zu_bufs = z_bufs[:n_c], z_bufs[n_c:]
    fc = zg_bufs[0].shape[1]
    f = n_c * fc
    tm = x_ref.shape[1]
    s8 = SUBLANES

    @pl.when(t == 0)
    def _():
        for j in range(n_c):
            zg_bufs[j][0:s8, :] = init_ref[0, 0, j]
            zu_bufs[j][0:s8, :] = init_ref[0, 1, j]

    x = x_ref[0]
    h = _rmsnorm(x, gn_ref[...]).astype(_BF16)

    def up(j):
        zg_bufs[j][s8:s8 + tm, :] = _dot(h, wup_ref[:, j * fc:(j + 1) * fc])
        zu_bufs[j][s8:s8 + tm, :] = _dot(h, wup_ref[:, f + j * fc:f + (j + 1) * fc])

    sub = lax.broadcasted_iota(jnp.int32, (1, s8, fc), 1)

    def conv(buf, c0):
        w = cw_ref[:, c0:c0 + fc]
        cur = buf[s8:s8 + tm, :].reshape(tm // s8, s8, fc)
        head = buf[0:s8, :].reshape(1, s8, fc)
        shifted = []
        for k in (1, 2):
            rot = pltpu.roll(cur, k, 1)
            prev = jnp.concatenate([pltpu.roll(head, k, 1), rot[:-1]], axis=0)
            shifted.append(jnp.where(sub < k, prev, rot))
        y = w[2:3] * cur + w[1:2] * shifted[0] + w[0:1] * shifted[1] + cb_ref[:, c0:c0 + fc]
        return y.reshape(tm, fc)

    def act(j):
        a = _silu(conv(zg_bufs[j], j * fc)) * conv(zu_bufs[j], f + j * fc)
        zg_bufs[j][0:s8, :] = zg_bufs[j][tm:tm + s8, :]
        zu_bufs[j][0:s8, :] = zu_bufs[j][tm:tm + s8, :]
        return a.astype(_BF16)

    for j in range(n_c):
        up(j)
    acts = [act(j) for j in range(n_c)]
    acc = x
    for j in range(n_c):
        acc = acc + _dot(acts[j], wd_ref[j * fc:(j + 1) * fc, :])
    if final_norm:
        acc = _rmsnorm(acc, gfin_ref[...])
    o_ref[0] = acc

    @pl.when(t == n_t - 1)
    def _():
        for j in range(n_c):
            cache_ref[0, 0, j] = zg_bufs[j][0:s8, :]
            cache_ref[0, 1, j] = zu_bufs[j][0:s8, :]


def _ffn_call(x, init, w, gfin, *, n_seq, tm, final_norm):
    _, seq, d = x.shape
    fc = FFN_COLS
    n_c = w["wd"].shape[0] // fc
    shared = init.shape[0] == 1 and n_seq > 1
    tail_shape = (n_c, SUBLANES, fc)
    out, cache = pl.pallas_call(
        functools.partial(_ffn_kernel, final_norm=final_norm),
        grid=(n_seq, seq // tm),
        in_specs=[
            _tile_spec(tm, d),
            _init_spec(init.shape, shared),
            _const_spec(w["gn"].shape),
            _const_spec(w["wup"].shape), _const_spec(w["cw"].shape), _const_spec(w["cb"].shape),
            _const_spec(w["wd"].shape),
            _const_spec(gfin.shape),
        ],
        out_specs=[
            _tile_spec(tm, d),
            pl.BlockSpec((1, 2) + tail_shape, lambda b, t: (b, 0, 0, 0, 0)),
        ],
        out_shape=[
            jax.ShapeDtypeStruct((n_seq, seq, d), _F32),
            jax.ShapeDtypeStruct((n_seq, 2) + tail_shape, _F32),
        ],
        scratch_shapes=[pltpu.VMEM((SUBLANES + tm, fc), _F32) for _ in range(2 * n_c)],
        compiler_params=_compiler_params(),
        name="conv_ffn",
    )(x, init, w["gn"], w["wup"], w["cw"], w["cb"], w["wd"], gfin)
    return out, cache


def _sc_kernel(x_ref, xn_ref, init_ref, gn_ref, win_ref, cw_ref, wout_ref, o_ref, cache_ref,
               tail, p_scr, *, pipelined):
    t = pl.program_id(1)
    n_t = pl.num_programs(1)
    tm = x_ref.shape[1]

    d = x_ref.shape[2]

    def project(h, slot):
        for i in range(3):
            p_scr[slot, i] = _dot(h, win_ref[:, i * d:(i + 1) * d])

    @pl.when(t == 0)
    def _():
        tail[...] = init_ref[0]
        project(_rmsnorm(x_ref[0], gn_ref[...]).astype(_BF16), 0)

    def step(cur):
        if pipelined:
            project(_rmsnorm(xn_ref[0], gn_ref[...]).astype(_BF16), 1 - cur)
        ci = p_scr[cur, 1] * p_scr[cur, 2]
        y = _causal_conv(ci, tail[...], cw_ref[...])
        tail[...] = ci[tm - SUBLANES:]
        o_ref[0] = x_ref[0] + _dot((p_scr[cur, 0] * y).astype(_BF16), wout_ref[...])

    if pipelined:
        parity = lax.rem(t, 2)

        @pl.when(parity == 0)
        def _():
            step(0)

        @pl.when(parity == 1)
        def _():
            step(1)
    else:
        step(0)

    @pl.when(t == n_t - 1)
    def _():
        cache_ref[0] = tail[...]


def _sc_call(x, init, w, *, n_seq, tm):
    _, seq, d = x.shape
    n_t = seq // tm
    shared = init.shape[0] == 1 and n_seq > 1
    out, cache = pl.pallas_call(
        functools.partial(_sc_kernel, pipelined=n_t > 1),
        grid=(n_seq, n_t),
        in_specs=[
            _tile_spec(tm, d), _next_tile_spec(tm, d, n_t),
            _init_spec(init.shape, shared),
            _const_spec(w["gn"].shape), _const_spec(w["win"].shape),
            _const_spec(w["cw"].shape), _const_spec(w["wout"].shape),
        ],
        out_specs=[
            _tile_spec(tm, d),
            pl.BlockSpec((1, SUBLANES, d), lambda b, t: (b, 0, 0)),
        ],
        out_shape=[
            jax.ShapeDtypeStruct((n_seq, seq, d), _F32),
            jax.ShapeDtypeStruct((n_seq, SUBLANES, d), _F32),
        ],
        scratch_shapes=[pltpu.VMEM((SUBLANES, d), _F32), pltpu.VMEM((2, 3, tm, d), _F32)],
        compiler_params=_compiler_params(),
        name="shortconv_mixer",
    )(x, x, init, w["gn"], w["win"], w["cw"], w["wout"])
    return out, cache


def _cumsum_rows(a):
    n = a.shape[0]
    rid = lax.broadcasted_iota(jnp.int32, (SUBLANES, a.shape[1]), 0)
    groups = []
    for i in range(0, n, SUBLANES):
        grp = a[i:i + SUBLANES]
        s = 1
        while s < SUBLANES:
            grp = grp + jnp.where(rid >= s, pltpu.roll(grp, s, 0), 0.0)
            s *= 2
        if groups:
            grp = grp + groups[-1][SUBLANES - 1:SUBLANES]
        groups.append(grp)
    return jnp.concatenate(groups, axis=0) if len(groups) > 1 else groups[0]


def _level_masks(c):
    tt = lax.broadcasted_iota(jnp.int32, (c, c), 0)
    ss = lax.broadcasted_iota(jnp.int32, (c, c), 1)
    txs = tt ^ ss
    lower = tt > ss
    masks = [tt == ss]
    half = 1
    while half < c:
        masks.append(lower & (txs >= half) & (txs < 2 * half))
        half *= 2
    return masks


def _gla_chunk_scores(q, k, g, c, masks, mild):
    n_h = GLA_HEADS
    dk = q.shape[1] // n_h
    b = _cumsum_rows(g)
    b_last = b[c - 1:c]
    qe = (q * jnp.exp(b)).astype(_BF16)
    kd = (k * jnp.exp(b_last - b)).astype(_BF16)
    decay = jnp.exp(b_last)

    if mild:
        k_inv = (k * jnp.exp(-b)).astype(_BF16)
        tt = lax.broadcasted_iota(jnp.int32, (c, c), 0)
        causal = tt >= lax.broadcasted_iota(jnp.int32, (c, c), 1)
        scores = []
        for hh in range(n_h):
            ks = slice(hh * dk, (hh + 1) * dk)
            a = jnp.where(causal, _dot_nt(qe[:, ks], k_inv[:, ks]), 0.0)
            scores.append(a.astype(_BF16))
        return dict(scores=scores, qe=qe, kd=kd, decay=decay)

    rid = lax.broadcasted_iota(jnp.int32, b.shape, 0)
    levels = [(q.astype(_BF16), k.astype(_BF16))]
    e_k = b
    half = 1
    while half < c:
        hi = (rid & half) != 0
        b_mid = jnp.where(hi, pltpu.roll(e_k, half, 0), e_k)
        scale = jnp.exp2((b - b_mid) * jnp.where(hi, _LOG2E, -_LOG2E))
        x = (jnp.where(hi, q, k) * scale).astype(_BF16)
        levels.append((x, x))
        if 2 * half < c:
            e_k = jnp.where(hi, e_k, pltpu.roll(e_k, c - half, 0))
        half *= 2

    scores = []
    for hh in range(n_h):
        ks = slice(hh * dk, (hh + 1) * dk)
        a = jnp.zeros((c, c), _F32)
        for mask, (qs, kk) in zip(masks, levels):
            a = jnp.where(mask, _dot_nt(qs[:, ks], kk[:, ks]), a)
        scores.append(a.astype(_BF16))
    return dict(scores=scores, qe=qe, kd=kd, decay=decay)


def _gla_chunk_apply(pre, v, s_ref, hg):
    n_h = GLA_HEADS
    qe, kd, decay = pre["qe"], pre["kd"], pre["decay"]
    dk = qe.shape[1] // n_h
    dv = v.shape[1] // n_h
    outs = []
    for hh in range(n_h):
        ks = slice(hh * dk, (hh + 1) * dk)
        vs = slice(hh * dv, (hh + 1) * dv)
        s_old = s_ref[hh]
        o = _dot(qe[:, ks], s_old.astype(_BF16)) + _dot(pre["scores"][hh], v[:, vs])
        o = o * lax.rsqrt(jnp.mean(o * o, axis=-1, keepdims=True) + EPS) * hg[:, vs]
        outs.append(o)
        dcol = jnp.transpose(jnp.broadcast_to(decay[:, ks], (dk, dk)))
        dmat = jnp.concatenate([dcol] * (dv // dk), axis=1)
        s_ref[hh] = dmat * s_old + _dot_tn(kd[:, ks], v[:, vs])
    return jnp.concatenate(outs, axis=1)


def _gla_kernel(x_ref, s0_ref, gn_ref, win_ref, wgd_ref, wgu_ref, bgate_ref, hg_ref, wout_ref,
                o_ref, sout_ref, s_scr, p_scr, g_scr, o_scr, h_scr, *, chunk):
    t = pl.program_id(1)
    n_t = pl.num_programs(1)
    tm = x_ref.shape[1]
    hk = g_scr.shape[1]
    hv = o_scr.shape[1]
    n_chunks = tm // chunk

    @pl.when(t == 0)
    def _():
        s_scr[...] = s0_ref[0]

    h_scr[...] = _rmsnorm(x_ref[0], gn_ref[...]).astype(_BF16)
    h = h_scr[...]
    p_scr[:, 0:2 * hk] = _dot(h, win_ref[:, 0:2 * hk])
    gdown = _dot(h, wgd_ref[...]).astype(_BF16)
    z = _dot(gdown, wgu_ref[...]) + bgate_ref[...]
    g = _log_sigmoid(z) * (1.0 / GLA_TAU)
    g_scr[...] = g
    chunk_sums = [jnp.sum(g[ci * chunk:(ci + 1) * chunk], axis=0, keepdims=True)
                  for ci in range(n_chunks)]
    lowest = jnp.min(functools.reduce(jnp.minimum, chunk_sums))
    q_scale = float(hk // GLA_HEADS) ** -0.5

    def step_body(mild):
        masks = _level_masks(chunk)
        hb = h_scr[...]
        half = hv // 2
        pieces = [(2 * hk + i * half, 2 * hk + (i + 1) * half) for i in range(4)]

        def project(i):
            c0, c1 = pieces[i]
            p_scr[:, c0:c1] = _dot(hb, win_ref[:, c0:c1])

        def scores(ci):
            rows = slice(ci * chunk, (ci + 1) * chunk)
            q = p_scr[rows, 0:hk] * q_scale
            k = p_scr[rows, hk:2 * hk]
            return _gla_chunk_scores(q, k, g_scr[rows, :], chunk, masks, mild)

        def apply(ci, pre):
            rows = slice(ci * chunk, (ci + 1) * chunk)
            v = p_scr[rows, 2 * hk:2 * hk + hv].astype(_BF16)
            o_scr[rows, :] = _gla_chunk_apply(pre, v, s_scr, hg_ref[...])

        project(0)
        pre = scores(0)
        project(1)
        for ci in range(n_chunks):
            nxt_pre = scores(ci + 1) if ci + 1 < n_chunks else None
            apply(ci, pre)
            if ci < 2:
                project(2 + ci)
            pre = nxt_pre
        for i in range(2 + min(2, n_chunks), 4):
            project(i)
        r = p_scr[:, 2 * hk + hv:2 * hk + 2 * hv]
        o_ref[0] = x_ref[0] + _dot((o_scr[...] * _silu(r)).astype(_BF16), wout_ref[...])

    is_mild = lowest >= MILD_LOG_DECAY

    @pl.when(is_mild)
    def _():
        step_body(True)

    @pl.when(jnp.logical_not(is_mild))
    def _():
        step_body(False)

    @pl.when(t == n_t - 1)
    def _():
        sout_ref[0] = s_scr[...]


def _gla_call(x, s0, w, *, n_seq, tm, chunk):
    _, seq, d = x.shape
    n_h, dk, dv = s0.shape[1:]
    hk, hv = n_h * dk, n_h * dv
    shared = s0.shape[0] == 1 and n_seq > 1
    out, s_out = pl.pallas_call(
        functools.partial(_gla_kernel, chunk=chunk),
        grid=(n_seq, seq // tm),
        in_specs=[
            _tile_spec(tm, d),
            _init_spec(s0.shape, shared),
            _const_spec(w["gn"].shape), _const_spec(w["win"].shape),
            _const_spec(w["wgd"].shape), _const_spec(w["wgu"].shape),
            _const_spec(w["bgate"].shape), _const_spec(w["hg"].shape),
            _const_spec(w["wout"].shape),
        ],
        out_specs=[
            _tile_spec(tm, d),
            pl.BlockSpec((1, n_h, dk, dv), lambda b, t: (b, 0, 0, 0)),
        ],
        out_shape=[
            jax.ShapeDtypeStruct((n_seq, seq, d), _F32),
            jax.ShapeDtypeStruct((n_seq, n_h, dk, dv), _F32),
        ],
        scratch_shapes=[
            pltpu.VMEM((n_h, dk, dv), _F32),
            pltpu.VMEM((tm, 2 * hk + 2 * hv), _F32),
            pltpu.VMEM((tm, hk), _F32),
            pltpu.VMEM((tm, hv), _F32),
            pltpu.VMEM((tm, d), _BF16),
        ],
        compiler_params=_compiler_params(),
        name="gla_mixer",
    )(x, s0, w["gn"], w["win"], w["wgd"], w["wgu"], w["bgate"], w["hg"], w["wout"])
    return out, s_out


SHORT_STRIDE = 32


def _short_spec(shape):
    nd = len(shape)
    return pl.BlockSpec(shape, lambda i: (0,) * nd, pipeline_mode=pl.Buffered(1))


def _short_params():
    return pltpu.CompilerParams(dimension_semantics=("arbitrary",), vmem_limit_bytes=VMEM_LIMIT_BYTES)


def _token_rows(n_rows, width, n_tok):
    pos = lax.broadcasted_iota(jnp.int32, (n_rows, width), 0) & (SHORT_STRIDE - 1)
    return pos >= SHORT_STRIDE - n_tok


def _roll_conv(z, w):
    return w[2:3] * z + w[1:2] * pltpu.roll(z, 1, 0) + w[0:1] * pltpu.roll(z, 2, 0)


def _last_rows(z, n_seg):
    return [z[(s + 1) * SHORT_STRIDE - SUBLANES:(s + 1) * SHORT_STRIDE] for s in range(n_seg)]


def _ffn_short_kernel(x_ref, inj_ref, gn_ref, wup_ref, cw_ref, cb_ref, wd_ref, gfin_ref,
                      o_ref, tail_ref, *, final_norm, n_tok):
    n_seg = tail_ref.shape[0]
    f = wd_ref.shape[0]
    fc = FFN_COLS
    x = x_ref[...]
    h = _rmsnorm(x, gn_ref[...]).astype(_BF16)
    acc = x
    for j in range(f // fc):
        halves = []
        for c0 in (j * fc, f + j * fc):
            cols = slice(c0, c0 + fc)
            z = _dot(h, wup_ref[:, cols]) + inj_ref[:, cols]
            for s, rows in enumerate(_last_rows(z, n_seg)):
                tail_ref[s, :, cols] = rows
            halves.append(_roll_conv(z, cw_ref[:, cols]) + cb_ref[:, cols])
        a = (_silu(halves[0]) * halves[1]).astype(_BF16)
        acc = acc + _dot(a, wd_ref[j * fc:(j + 1) * fc, :])
    if final_norm:
        acc = _rmsnorm(acc, gfin_ref[...])
    o_ref[...] = jnp.where(_token_rows(x.shape[0], x.shape[1], n_tok), acc, 0.0)


def _ffn_short_call(x, inj, w, gfin, *, n_seg, n_tok, final_norm):
    rows, d = x.shape
    f2 = w["wup"].shape[1]
    args = (x, inj, w["gn"], w["wup"], w["cw"], w["cb"], w["wd"], gfin)
    return pl.pallas_call(
        functools.partial(_ffn_short_kernel, final_norm=final_norm, n_tok=n_tok),
        grid=(1,),
        in_specs=[_short_spec(a.shape) for a in args],
        out_specs=[_short_spec((rows, d)), _short_spec((n_seg, SUBLANES, f2))],
        out_shape=[jax.ShapeDtypeStruct((rows, d), _F32),
                   jax.ShapeDtypeStruct((n_seg, SUBLANES, f2), _F32)],
        compiler_params=_short_params(),
        name="conv_ffn_short",
    )(*args)


def _sc_short_kernel(x_ref, inj_ref, gn_ref, win_ref, cw_ref, wout_ref, o_ref, tail_ref, *, n_tok):
    n_seg = tail_ref.shape[0]
    x = x_ref[...]
    d = x.shape[1]
    h = _rmsnorm(x, gn_ref[...]).astype(_BF16)
    bg = _dot(h, win_ref[:, 0:d])
    ci = _dot(h, win_ref[:, d:2 * d]) * _dot(h, win_ref[:, 2 * d:3 * d]) + inj_ref[...]
    for s, rows in enumerate(_last_rows(ci, n_seg)):
        tail_ref[s] = rows
    y = _roll_conv(ci, cw_ref[...])
    out = x + _dot((bg * y).astype(_BF16), wout_ref[...])
    o_ref[...] = jnp.where(_token_rows(x.shape[0], d, n_tok), out, 0.0)


def _sc_short_call(x, inj, w, *, n_seg, n_tok):
    rows, d = x.shape
    args = (x, inj, w["gn"], w["win"], w["cw"], w["wout"])
    return pl.pallas_call(
        functools.partial(_sc_short_kernel, n_tok=n_tok),
        grid=(1,),
        in_specs=[_short_spec(a.shape) for a in args],
        out_specs=[_short_spec((rows, d)), _short_spec((n_seg, SUBLANES, d))],
        out_shape=[jax.ShapeDtypeStruct((rows, d), _F32),
                   jax.ShapeDtypeStruct((n_seg, SUBLANES, d), _F32)],
        compiler_params=_short_params(),
        name="shortconv_mixer_short",
    )(*args)


def _gla_short_kernel(x_ref, s0_ref, gn_ref, win_ref, wgd_ref, wgu_ref, bgate_ref, hg_ref, wout_ref,
                      o_ref, sout_ref, s_scr, p_scr, g_scr, o_scr, *, n_tok):
    n_seg = s0_ref.shape[0]
    c = SHORT_STRIDE
    hk = g_scr.shape[1]
    hv = o_scr.shape[1]
    x = x_ref[...]
    h = _rmsnorm(x, gn_ref[...]).astype(_BF16)
    p_scr[...] = _dot(h, win_ref[...])
    gdown = _dot(h, wgd_ref[...]).astype(_BF16)
    z = _dot(gdown, wgu_ref[...]) + bgate_ref[...]
    g = jnp.where(_token_rows(x.shape[0], hk, n_tok), _log_sigmoid(z) * (1.0 / GLA_TAU), 0.0)
    g_scr[...] = g
    sums = [jnp.sum(g[s * c:(s + 1) * c], axis=0, keepdims=True) for s in range(n_seg)]
    is_mild = jnp.min(functools.reduce(jnp.minimum, sums)) >= MILD_LOG_DECAY
    q_scale = float(hk // GLA_HEADS) ** -0.5
    masks = _level_masks(c)

    def walk(mild):
        def body(s, carry):
            rows = pl.ds(pl.multiple_of(s * c, c), c)
            q = p_scr[rows, 0:hk] * q_scale
            k = p_scr[rows, hk:2 * hk]
            v = p_scr[rows, 2 * hk:2 * hk + hv].astype(_BF16)
            s_scr[...] = s0_ref[s]
            pre = _gla_chunk_scores(q, k, g_scr[rows, :], c, masks, mild)
            o_scr[rows, :] = _gla_chunk_apply(pre, v, s_scr, hg_ref[...])
            sout_ref[s] = s_scr[...]
            return carry

        lax.fori_loop(0, n_seg, body, 0)

    @pl.when(is_mild)
    def _():
        walk(True)

    @pl.when(jnp.logical_not(is_mild))
    def _():
        walk(False)

    r = p_scr[:, 2 * hk + hv:2 * hk + 2 * hv]
    out = x + _dot((o_scr[...] * _silu(r)).astype(_BF16), wout_ref[...])
    o_ref[...] = jnp.where(_token_rows(x.shape[0], x.shape[1], n_tok), out, 0.0)


def _gla_short_call(x, s0, w, *, n_tok):
    rows, d = x.shape
    n_seg, n_h, dk, dv = s0.shape
    hk, hv = n_h * dk, n_h * dv
    args = (x, s0, w["gn"], w["win"], w["wgd"], w["wgu"], w["bgate"], w["hg"], w["wout"])
    return pl.pallas_call(
        functools.partial(_gla_short_kernel, n_tok=n_tok),
        grid=(1,),
        in_specs=[_short_spec(a.shape) for a in args],
        out_specs=[_short_spec((rows, d)), _short_spec(s0.shape)],
        out_shape=[jax.ShapeDtypeStruct((rows, d), _F32), jax.ShapeDtypeStruct(s0.shape, _F32)],
        scratch_shapes=[
            pltpu.VMEM((n_h, dk, dv), _F32),
            pltpu.VMEM((rows, 2 * hk + 2 * hv), _F32),
            pltpu.VMEM((rows, hk), _F32),
            pltpu.VMEM((rows, hv), _F32),
        ],
        compiler_params=_short_params(),
        name="gla_mixer_short",
    )(*args)


def _short_rows(a):
    n_seg, n_tok, width = a.shape
    return jnp.pad(a, ((0, 0), (SHORT_STRIDE - n_tok, 0), (0, 0))).reshape(n_seg * SHORT_STRIDE, width)


def _short_inject(cache, n_tok):
    n_seg, n_prev, width = cache.shape
    lead = SHORT_STRIDE - n_tok - n_prev
    return jnp.pad(cache, ((0, 0), (lead, n_tok), (0, 0))).reshape(n_seg * SHORT_STRIDE, width)


def _short_trunk(xs, gla_s, conv_c, ffn_c, params, gfin):
    n_seg, n_tok, d = xs.shape
    depth = len(params)
    x = _short_rows(xs)
    keep = lambda tail: tail[:, SUBLANES - (CONV_W - 1):, :]
    gla_new, conv_new, ffn_new = [], [], []
    for i, (mix, ffn) in enumerate(params):
        if i % 2 == 0:
            x, s = _gla_short_call(x, gla_s[i // 2], mix, n_tok=n_tok)
            gla_new.append(s)
        else:
            x, tail = _sc_short_call(x, _short_inject(conv_c[i // 2], n_tok), mix,
                                     n_seg=n_seg, n_tok=n_tok)
            conv_new.append(keep(tail))
        x, tail = _ffn_short_call(x, _short_inject(ffn_c[i], n_tok), ffn, gfin, n_seg=n_seg,
                                  n_tok=n_tok, final_norm=(i == depth - 1))
        ffn_new.append(keep(tail))
    y = x.reshape(n_seg, SHORT_STRIDE, d)[:, SHORT_STRIDE - n_tok:, :]
    return y, gla_new, conv_new, ffn_new


def _row(v):
    return v.reshape(1, -1).astype(_F32)


def _prep_gla(norm, w_in, w_gate_up, b_gate, head_gain, w_out):
    main = w_in.shape[1] - GLA_GATE_RANK
    wgd = jnp.pad(w_in[:, main:], ((0, 0), (0, LANES - GLA_GATE_RANK)))
    wgu = jnp.pad(w_gate_up, ((0, LANES - GLA_GATE_RANK), (0, 0)))
    return dict(gn=_row(norm), win=w_in[:, :main].astype(_BF16), wgd=wgd.astype(_BF16),
                wgu=wgu.astype(_BF16), bgate=_row(b_gate), hg=_row(head_gain),
                wout=w_out.astype(_BF16))


def _prep_sc(norm, w_in, conv_w, w_out):
    return dict(gn=_row(norm), win=w_in.astype(_BF16), cw=conv_w.astype(_F32),
                wout=w_out.astype(_BF16))


def _prep_ffn(norm, w_up, conv_w, conv_b, w_down):
    assert w_down.shape[0] % FFN_COLS == 0
    return dict(gn=_row(norm), wup=w_up.astype(_BF16), cw=conv_w.astype(_F32), cb=_row(conv_b),
                wd=w_down.astype(_BF16))


def _tail_from_cache(cache):
    return jnp.pad(cache, ((0, 0), (SUBLANES - (CONV_W - 1), 0), (0, 0)))


def _ffn_tail_from_cache(cache, n_c, fc):
    b = cache.shape[0]
    t = _tail_from_cache(cache).reshape(b, SUBLANES, 2, n_c, fc)
    return t.transpose(0, 2, 3, 1, 4)


def _ffn_cache_from_tail(tail):
    b = tail.shape[0]
    rows = tail[:, :, :, SUBLANES - (CONV_W - 1):, :]
    return rows.transpose(0, 3, 1, 2, 4).reshape(b, CONV_W - 1, -1)


def _trunk(x, gla_s, conv_tail, ffn_tail, params, gfin, *, tm_gla, tm_sc, tm_ffn, chunk):
    n_seq = x.shape[0]
    depth = len(params)
    gla_new, conv_new, ffn_new = [], [], []
    i_gla = i_conv = 0
    for i, (mix, ffn) in enumerate(params):
        if i % 2 == 0:
            x, s = _gla_call(x, gla_s[i_gla], mix, n_seq=n_seq, tm=tm_gla, chunk=chunk)
            gla_new.append(s)
            i_gla += 1
        else:
            x, s = _sc_call(x, conv_tail[i_conv], mix, n_seq=n_seq, tm=tm_sc)
            conv_new.append(s)
            i_conv += 1
        x, s = _ffn_call(x, ffn_tail[i], ffn, gfin, n_seq=n_seq, tm=tm_ffn,
                         final_norm=(i == depth - 1))
        ffn_new.append(s)
    return x, gla_new, conv_new, ffn_new


def kernel(x_prompt, x_sample, state_gla, cache_conv, cache_ffn, meta, norm_mix, norm_ffn, norm_final,
           gla_w_in, gla_w_gate_up, gla_b_gate, gla_head_gain, gla_w_out,
           sc_w_in, sc_conv_w, sc_w_out, ffn_w_up, ffn_conv_w, ffn_conv_b, ffn_w_down):
    depth = norm_mix.shape[0]
    n_b, seq, d = x_prompt.shape
    n_s, s_len, _ = x_sample.shape
    assert s_len == N_META and seq % GLA_CHUNK == 0
    f = ffn_w_down.shape[1]
    n_c, fc = f // FFN_COLS, FFN_COLS

    params = []
    for i in range(depth):
        j = i // 2
        if i % 2 == 0:
            mix = _prep_gla(norm_mix[i], gla_w_in[j], gla_w_gate_up[j], gla_b_gate[j],
                            gla_head_gain[j], gla_w_out[j])
        else:
            mix = _prep_sc(norm_mix[i], sc_w_in[j], sc_conv_w[j], sc_w_out[j])
        params.append((mix, _prep_ffn(norm_ffn[i], ffn_w_up[i], ffn_conv_w[i], ffn_conv_b[i],
                                      ffn_w_down[i])))
    gfin = _row(norm_final)

    xs = jnp.concatenate([x_sample, meta[None].astype(x_sample.dtype)], axis=0)
    zero1 = lambda a: jnp.zeros((a.shape[0], 1) + a.shape[2:], a.dtype)
    gla_s = jnp.concatenate([state_gla, zero1(state_gla)], axis=1).astype(_F32)
    conv_c = jnp.concatenate([cache_conv, zero1(cache_conv)], axis=1).astype(_F32)
    ffn_c = jnp.concatenate([cache_ffn, zero1(cache_ffn)], axis=1).astype(_F32)
    ys, gla_1, conv_1, ffn_1 = _short_trunk(xs, gla_s, conv_c, ffn_c, params, gfin)

    yp, gla_2, conv_2, ffn_2 = _trunk(
        x_prompt, [s[n_s:] for s in gla_1],
        [_tail_from_cache(s[n_s:]) for s in conv_1],
        [_ffn_tail_from_cache(s[n_s:], n_c, fc) for s in ffn_1],
        params, gfin, tm_gla=min(TILE_GLA, seq), tm_sc=min(TILE_SC, seq), tm_ffn=min(TILE_FFN, seq),
        chunk=GLA_CHUNK)

    dt = x_prompt.dtype
    tail2 = lambda s: s[:, SUBLANES - (CONV_W - 1):, :]
    return (yp.astype(dt), ys[:n_s].astype(dt),
            jnp.stack(gla_2).astype(dt), jnp.stack([s[:n_s] for s in gla_1]).astype(dt),
            jnp.stack([tail2(s) for s in conv_2]).astype(dt),
            jnp.stack([s[:n_s] for s in conv_1]).astype(dt),
            jnp.stack([_ffn_cache_from_tail(s) for s in ffn_2]).astype(dt),
            jnp.stack([s[:n_s] for s in ffn_1]).astype(dt))
```

```python
import functools

import jax
import jax.numpy as jnp
from jax import lax
from jax.experimental import pallas as pl
from jax.experimental.pallas import tpu as pltpu

N_META = 16
GLA_HEADS = 4
GLA_GATE_RANK = 16
GLA_TAU = 16.0
CONV_W = 3
EPS = 1e-6
GLA_CHUNK = 64
MILD_LOG_DECAY = -60.0
FFN_COLS = 256
SC_COLS = 256
TILE_GLA = 512
TILE_SC = 512
TILE_FFN = 512
LANES = 128
SUBLANES = 8
VMEM_LIMIT_BYTES = 56 * 1024 * 1024

_BF16 = jnp.bfloat16
_F32 = jnp.float32
_LOG2E = 1.4426950408889634


def _dot(a, b):
    return lax.dot_general(a, b, (((1,), (0,)), ((), ())), preferred_element_type=_F32)


def _dot_nt(a, b):
    return lax.dot_general(a, b, (((1,), (1,)), ((), ())), preferred_element_type=_F32)


def _dot_tn(a, b):
    return lax.dot_general(a, b, (((0,), (0,)), ((), ())), preferred_element_type=_F32)


def _rmsnorm(x, g):
    return x * lax.rsqrt(jnp.mean(x * x, axis=-1, keepdims=True) + EPS) * g


def _silu(x):
    return x / (1.0 + jnp.exp2(x * (-_LOG2E)))


def _log_sigmoid(z):
    return jnp.minimum(z, 0.0) - jnp.log(1.0 + jnp.exp(-jnp.abs(z)))


def _shift_rows(z, tail, k):
    rolled = pltpu.roll(z, k, 0)
    head = rolled[:SUBLANES]
    rid = lax.broadcasted_iota(jnp.int32, head.shape, 0)
    head = jnp.where(rid < k, pltpu.roll(tail, k, 0), head)
    if z.shape[0] == SUBLANES:
        return head
    return jnp.concatenate([head, rolled[SUBLANES:]], axis=0)


def _causal_conv(z, tail, w):
    return w[2:3] * z + w[1:2] * _shift_rows(z, tail, 1) + w[0:1] * _shift_rows(z, tail, 2)


def _const_spec(shape):
    nd = len(shape)
    return pl.BlockSpec(shape, lambda b, t: (0,) * nd, pipeline_mode=pl.Buffered(1))


def _init_spec(shape, shared):
    nd = len(shape)
    block = (1,) + tuple(shape[1:])
    if shared:
        return pl.BlockSpec(block, lambda b, t: (0,) * nd)
    return pl.BlockSpec(block, lambda b, t: (b,) + (0,) * (nd - 1))


def _tile_spec(tm, d):
    return pl.BlockSpec((1, tm, d), lambda b, t: (b, t, 0))


def _next_tile_spec(tm, d, n_t):
    return pl.BlockSpec((1, tm, d), lambda b, t: (b, jnp.minimum(t + 1, n_t - 1), 0))


def _compiler_params(flags=None):
    return pltpu.CompilerParams(dimension_semantics=("arbitrary", "arbitrary"),
                                vmem_limit_bytes=VMEM_LIMIT_BYTES, flags=flags)


def _ffn_kernel(x_ref, init_ref, gn_ref, wup_ref, cw_ref, cb_ref, wd_ref,
                gfin_ref, o_ref, cache_ref, *z_bufs, final_norm):
    t = pl.program_id(1)
    n_t = pl.num_programs(1)
    n_c = len(z_bufs) // 2
    zg_bufs, zu_bufs = z_bufs[:n_c], z_bufs[n_c:]
    fc = zg_bufs[0].shape[1]
    f = n_c * fc
    tm = x_ref.shape[1]
    s8 = SUBLANES

    @pl.when(t == 0)
    def _():
        for j in range(n_c):
            zg_bufs[j][0:s8, :] = init_ref[0, 0, j]
            zu_bufs[j][0:s8, :] = init_ref[0, 1, j]

    x = x_ref[0]
    h = _rmsnorm(x, gn_ref[...]).astype(_BF16)

    def up(j):
        zg_bufs[j][s8:s8 + tm, :] = _dot(h, wup_ref[:, j * fc:(j + 1) * fc])
        zu_bufs[j][s8:s8 + tm, :] = _dot(h, wup_ref[:, f + j * fc:f + (j + 1) * fc])

    sub = lax.broadcasted_iota(jnp.int32, (1, s8, fc), 1)

    def conv(buf, c0):
        w = cw_ref[:, c0:c0 + fc]
        cur = buf[s8:s8 + tm, :].reshape(tm // s8, s8, fc)
        head = buf[0:s8, :].reshape(1, s8, fc)
        shifted = []
        for k in (1, 2):
            rot = pltpu.roll(cur, k, 1)
            prev = jnp.concatenate([pltpu.roll(head, k, 1), rot[:-1]], axis=0)
            shifted.append(jnp.where(sub < k, prev, rot))
        y = w[2:3] * cur + w[1:2] * shifted[0] + w[0:1] * shifted[1] + cb_ref[:, c0:c0 + fc]
        return y.reshape(tm, fc)

    def act(j):
        a = _silu(conv(zg_bufs[j], j * fc)) * conv(zu_bufs[j], f + j * fc)
        zg_bufs[j][0:s8, :] = zg_bufs[j][tm:tm + s8, :]
        zu_bufs[j][0:s8, :] = zu_bufs[j][tm:tm + s8, :]
        return a.astype(_BF16)

    for j in range(n_c):
        up(j)
    acts = [act(j) for j in range(n_c)]
    acc = x
    for j in range(n_c):
        acc = acc + _dot(acts[j], wd_ref[j * fc:(j + 1) * fc, :])
    if final_norm:
        acc = _rmsnorm(acc, gfin_ref[...])
    o_ref[0] = acc

    @pl.when(t == n_t - 1)
    def _():
        for j in range(n_c):
            cache_ref[0, 0, j] = zg_bufs[j][0:s8, :]
            cache_ref[0, 1, j] = zu_bufs[j][0:s8, :]


def _ffn_call(x, init, w, gfin, *, n_seq, tm, final_norm):
    _, seq, d = x.shape
    fc = FFN_COLS
    n_c = w["wd"].shape[0] // fc
    shared = init.shape[0] == 1 and n_seq > 1
    tail_shape = (n_c, SUBLANES, fc)
    out, cache = pl.pallas_call(
        functools.partial(_ffn_kernel, final_norm=final_norm),
        grid=(n_seq, seq // tm),
        in_specs=[
            _tile_spec(tm, d),
            _init_spec(init.shape, shared),
            _const_spec(w["gn"].shape),
            _const_spec(w["wup"].shape), _const_spec(w["cw"].shape), _const_spec(w["cb"].shape),
            _const_spec(w["wd"].shape),
            _const_spec(gfin.shape),
        ],
        out_specs=[
            _tile_spec(tm, d),
            pl.BlockSpec((1, 2) + tail_shape, lambda b, t: (b, 0, 0, 0, 0)),
        ],
        out_shape=[
            jax.ShapeDtypeStruct((n_seq, seq, d), _F32),
            jax.ShapeDtypeStruct((n_seq, 2) + tail_shape, _F32),
        ],
        scratch_shapes=[pltpu.VMEM((SUBLANES + tm, fc), _F32) for _ in range(2 * n_c)],
        compiler_params=_compiler_params(),
        name="conv_ffn",
    )(x, init, w["gn"], w["wup"], w["cw"], w["cb"], w["wd"], gfin)
    return out, cache


def _sc_kernel(x_ref, init_ref, gn_ref, win_ref, cw_ref, wout_ref, o_ref, cache_ref, tail):
    t = pl.program_id(1)
    n_t = pl.num_programs(1)
    tm, d = x_ref.shape[1], x_ref.shape[2]
    wc = SC_COLS

    @pl.when(t == 0)
    def _():
        tail[...] = init_ref[0]

    x = x_ref[0]
    h = _rmsnorm(x, gn_ref[...]).astype(_BF16)
    blocks = [slice(c0, c0 + wc) for c0 in range(0, d, wc)]
    proj = [[_dot(h, win_ref[:, i * d + c.start:i * d + c.stop]) for i in range(3)] for c in blocks]
    gated = []
    for c, (p_b, p_c, p_u) in zip(blocks, proj):
        ci = p_c * p_u
        y = _causal_conv(ci, tail[:, c], cw_ref[:, c])
        tail[:, c] = ci[tm - SUBLANES:]
        gated.append((p_b * y).astype(_BF16))
    acc = x
    for c, a in zip(blocks, gated):
        acc = acc + _dot(a, wout_ref[c, :])
    o_ref[0] = acc

    @pl.when(t == n_t - 1)
    def _():
        cache_ref[0] = tail[...]


def _sc_call(x, init, w, *, n_seq, tm):
    _, seq, d = x.shape
    shared = init.shape[0] == 1 and n_seq > 1
    out, cache = pl.pallas_call(
        _sc_kernel,
        grid=(n_seq, seq // tm),
        in_specs=[
            _tile_spec(tm, d),
            _init_spec(init.shape, shared),
            _const_spec(w["gn"].shape), _const_spec(w["win"].shape),
            _const_spec(w["cw"].shape), _const_spec(w["wout"].shape),
        ],
        out_specs=[
            _tile_spec(tm, d),
            pl.BlockSpec((1, SUBLANES, d), lambda b, t: (b, 0, 0)),
        ],
        out_shape=[
            jax.ShapeDtypeStruct((n_seq, seq, d), _F32),
            jax.ShapeDtypeStruct((n_seq, SUBLANES, d), _F32),
        ],
        scratch_shapes=[pltpu.VMEM((SUBLANES, d), _F32)],
        compiler_params=_compiler_params(),
        name="shortconv_mixer",
    )(x, init, w["gn"], w["win"], w["cw"], w["wout"])
    return out, cache


def _cumsum_rows(a):
    n = a.shape[0]
    rid = lax.broadcasted_iota(jnp.int32, (SUBLANES, a.shape[1]), 0)
    groups = []
    for i in range(0, n, SUBLANES):
        grp = a[i:i + SUBLANES]
        s = 1
        while s < SUBLANES:
            grp = grp + jnp.where(rid >= s, pltpu.roll(grp, s, 0), 0.0)
            s *= 2
        if groups:
            grp = grp + groups[-1][SUBLANES - 1:SUBLANES]
        groups.append(grp)
    return jnp.concatenate(groups, axis=0) if len(groups) > 1 else groups[0]


def _level_masks(c):
    tt = lax.broadcasted_iota(jnp.int32, (c, c), 0)
    ss = lax.broadcasted_iota(jnp.int32, (c, c), 1)
    txs = tt ^ ss
    lower = tt > ss
    masks = [tt == ss]
    half = 1
    while half < c:
        masks.append(lower & (txs >= half) & (txs < 2 * half))
        half *= 2
    return masks


def _gla_chunk_scores(q, k, g, c, masks, mild):
    n_h = GLA_HEADS
    dk = q.shape[1] // n_h
    b = _cumsum_rows(g)
    b_last = b[c - 1:c]
    qe = (q * jnp.exp(b)).astype(_BF16)
    kd = (k * jnp.exp(b_last - b)).astype(_BF16)
    decay = jnp.exp(b_last)

    if mild:
        k_inv = (k * jnp.exp(-b)).astype(_BF16)
        tt = lax.broadcasted_iota(jnp.int32, (c, c), 0)
        causal = tt >= lax.broadcasted_iota(jnp.int32, (c, c), 1)
        scores = []
        for hh in range(n_h):
            ks = slice(hh * dk, (hh + 1) * dk)
            a = jnp.where(causal, _dot_nt(qe[:, ks], k_inv[:, ks]), 0.0)
            scores.append(a.astype(_BF16))
        return dict(scores=scores, qe=qe, kd=kd, decay=decay)

    rid = lax.broadcasted_iota(jnp.int32, b.shape, 0)
    levels = [(q.astype(_BF16), k.astype(_BF16))]
    e_k = b
    half = 1
    while half < c:
        hi = (rid & half) != 0
        b_mid = jnp.where(hi, pltpu.roll(e_k, half, 0), e_k)
        scale = jnp.exp2((b - b_mid) * jnp.where(hi, _LOG2E, -_LOG2E))
        x = (jnp.where(hi, q, k) * scale).astype(_BF16)
        levels.append((x, x))
        if 2 * half < c:
            e_k = jnp.where(hi, e_k, pltpu.roll(e_k, c - half, 0))
        half *= 2

    scores = []
    for hh in range(n_h):
        ks = slice(hh * dk, (hh + 1) * dk)
        a = jnp.zeros((c, c), _F32)
        for mask, (qs, kk) in zip(masks, levels):
            a = jnp.where(mask, _dot_nt(qs[:, ks], kk[:, ks]), a)
        scores.append(a.astype(_BF16))
    return dict(scores=scores, qe=qe, kd=kd, decay=decay)


def _gla_chunk_apply(pre, v, s_ref, hg):
    n_h = GLA_HEADS
    qe, kd, decay = pre["qe"], pre["kd"], pre["decay"]
    dk = qe.shape[1] // n_h
    dv = v.shape[1] // n_h
    outs = []
    for hh in range(n_h):
        ks = slice(hh * dk, (hh + 1) * dk)
        vs = slice(hh * dv, (hh + 1) * dv)
        s_old = s_ref[hh]
        o = _dot(qe[:, ks], s_old.astype(_BF16)) + _dot(pre["scores"][hh], v[:, vs])
        o = o * lax.rsqrt(jnp.mean(o * o, axis=-1, keepdims=True) + EPS) * hg[:, vs]
        outs.append(o)
        dcol = jnp.transpose(jnp.broadcast_to(decay[:, ks], (dk, dk)))
        dmat = jnp.concatenate([dcol] * (dv // dk), axis=1)
        s_ref[hh] = dmat * s_old + _dot_tn(kd[:, ks], v[:, vs])
    return jnp.concatenate(outs, axis=1)


def _gla_kernel(x_ref, s0_ref, gn_ref, win_ref, wgd_ref, wgu_ref, bgate_ref, hg_ref, wout_ref,
                o_ref, sout_ref, s_scr, p_scr, g_scr, o_scr, h_scr, *, chunk):
    t = pl.program_id(1)
    n_t = pl.num_programs(1)
    tm = x_ref.shape[1]
    hk = g_scr.shape[1]
    hv = o_scr.shape[1]
    n_chunks = tm // chunk

    @pl.when(t == 0)
    def _():
        s_scr[...] = s0_ref[0]

    h_scr[...] = _rmsnorm(x_ref[0], gn_ref[...]).astype(_BF16)
    h = h_scr[...]
    gdown = _dot(h, wgd_ref[...]).astype(_BF16)
    z = _dot(gdown, wgu_ref[...]) + bgate_ref[...]
    g = _log_sigmoid(z) * (1.0 / GLA_TAU)
    g_scr[...] = g
    p_scr[:, 0:2 * hk] = _dot(h, win_ref[:, 0:2 * hk])
    chunk_sums = [jnp.sum(g[ci * chunk:(ci + 1) * chunk], axis=0, keepdims=True)
                  for ci in range(n_chunks)]
    lowest = jnp.min(functools.reduce(jnp.minimum, chunk_sums))
    q_scale = float(hk // GLA_HEADS) ** -0.5

    def step_body(mild):
        masks = _level_masks(chunk)
        hb = h_scr[...]
        half = hv // 2
        pieces = [(2 * hk + i * half, 2 * hk + (i + 1) * half) for i in range(4)]

        def project(i):
            c0, c1 = pieces[i]
            p_scr[:, c0:c1] = _dot(hb, win_ref[:, c0:c1])

        def scores(ci):
            rows = slice(ci * chunk, (ci + 1) * chunk)
            q = p_scr[rows, 0:hk] * q_scale
            k = p_scr[rows, hk:2 * hk]
            return _gla_chunk_scores(q, k, g_scr[rows, :], chunk, masks, mild)

        def apply(ci, pre):
            rows = slice(ci * chunk, (ci + 1) * chunk)
            v = p_scr[rows, 2 * hk:2 * hk + hv].astype(_BF16)
            o_scr[rows, :] = _gla_chunk_apply(pre, v, s_scr, hg_ref[...])

        project(0)
        pre = scores(0)
        project(1)
        for ci in range(n_chunks):
            nxt_pre = scores(ci + 1) if ci + 1 < n_chunks else None
            apply(ci, pre)
            if ci < 2:
                project(2 + ci)
            pre = nxt_pre
        for i in range(2 + min(2, n_chunks), 4):
            project(i)
        r = p_scr[:, 2 * hk + hv:2 * hk + 2 * hv]
        o_ref[0] = x_ref[0] + _dot((o_scr[...] * _silu(r)).astype(_BF16), wout_ref[...])

    is_mild = lowest >= MILD_LOG_DECAY

    @pl.when(is_mild)
    def _():
        step_body(True)

    @pl.when(jnp.logical_not(is_mild))
    def _():
        step_body(False)

    @pl.when(t == n_t - 1)
    def _():
        sout_ref[0] = s_scr[...]


def _gla_call(x, s0, w, *, n_seq, tm, chunk):
    _, seq, d = x.shape
    n_h, dk, dv = s0.shape[1:]
    hk, hv = n_h * dk, n_h * dv
    shared = s0.shape[0] == 1 and n_seq > 1
    out, s_out = pl.pallas_call(
        functools.partial(_gla_kernel, chunk=chunk),
        grid=(n_seq, seq // tm),
        in_specs=[
            _tile_spec(tm, d),
            _init_spec(s0.shape, shared),
            _const_spec(w["gn"].shape), _const_spec(w["win"].shape),
            _const_spec(w["wgd"].shape), _const_spec(w["wgu"].shape),
            _const_spec(w["bgate"].shape), _const_spec(w["hg"].shape),
            _const_spec(w["wout"].shape),
        ],
        out_specs=[
            _tile_spec(tm, d),
            pl.BlockSpec((1, n_h, dk, dv), lambda b, t: (b, 0, 0, 0)),
        ],
        out_shape=[
            jax.ShapeDtypeStruct((n_seq, seq, d), _F32),
            jax.ShapeDtypeStruct((n_seq, n_h, dk, dv), _F32),
        ],
        scratch_shapes=[
            pltpu.VMEM((n_h, dk, dv), _F32),
            pltpu.VMEM((tm, 2 * hk + 2 * hv), _F32),
            pltpu.VMEM((tm, hk), _F32),
            pltpu.VMEM((tm, hv), _F32),
            pltpu.VMEM((tm, d), _BF16),
        ],
        compiler_params=_compiler_params(),
        name="gla_mixer",
    )(x, s0, w["gn"], w["win"], w["wgd"], w["wgu"], w["bgate"], w["hg"], w["wout"])
    return out, s_out


SHORT_STRIDE = 32


def _short_spec(shape):
    nd = len(shape)
    return pl.BlockSpec(shape, lambda i: (0,) * nd, pipeline_mode=pl.Buffered(1))


def _short_params():
    return pltpu.CompilerParams(dimension_semantics=("arbitrary",), vmem_limit_bytes=VMEM_LIMIT_BYTES)


def _token_rows(n_rows, width, n_tok):
    pos = lax.broadcasted_iota(jnp.int32, (n_rows, width), 0) & (SHORT_STRIDE - 1)
    return pos >= SHORT_STRIDE - n_tok


def _roll_conv(z, w):
    return w[2:3] * z + w[1:2] * pltpu.roll(z, 1, 0) + w[0:1] * pltpu.roll(z, 2, 0)


def _last_rows(z, n_seg):
    return [z[(s + 1) * SHORT_STRIDE - SUBLANES:(s + 1) * SHORT_STRIDE] for s in range(n_seg)]


def _ffn_short_kernel(x_ref, inj_ref, gn_ref, wup_ref, cw_ref, cb_ref, wd_ref, gfin_ref,
                      o_ref, tail_ref, *, final_norm, n_tok):
    n_seg = tail_ref.shape[0]
    f = wd_ref.shape[0]
    fc = FFN_COLS
    x = x_ref[...]
    h = _rmsnorm(x, gn_ref[...]).astype(_BF16)
    acc = x
    for j in range(f // fc):
        halves = []
        for c0 in (j * fc, f + j * fc):
            cols = slice(c0, c0 + fc)
            z = _dot(h, wup_ref[:, cols]) + inj_ref[:, cols]
            for s, rows in enumerate(_last_rows(z, n_seg)):
                tail_ref[s, :, cols] = rows
            halves.append(_roll_conv(z, cw_ref[:, cols]) + cb_ref[:, cols])
        a = (_silu(halves[0]) * halves[1]).astype(_BF16)
        acc = acc + _dot(a, wd_ref[j * fc:(j + 1) * fc, :])
    if final_norm:
        acc = _rmsnorm(acc, gfin_ref[...])
    o_ref[...] = jnp.where(_token_rows(x.shape[0], x.shape[1], n_tok), acc, 0.0)


def _ffn_short_call(x, inj, w, gfin, *, n_seg, n_tok, final_norm):
    rows, d = x.shape
    f2 = w["wup"].shape[1]
    args = (x, inj, w["gn"], w["wup"], w["cw"], w["cb"], w["wd"], gfin)
    return pl.pallas_call(
        functools.partial(_ffn_short_kernel, final_norm=final_norm, n_tok=n_tok),
        grid=(1,),
        in_specs=[_short_spec(a.shape) for a in args],
        out_specs=[_short_spec((rows, d)), _short_spec((n_seg, SUBLANES, f2))],
        out_shape=[jax.ShapeDtypeStruct((rows, d), _F32),
                   jax.ShapeDtypeStruct((n_seg, SUBLANES, f2), _F32)],
        compiler_params=_short_params(),
        name="conv_ffn_short",
    )(*args)


def _sc_short_kernel(x_ref, inj_ref, gn_ref, win_ref, cw_ref, wout_ref, o_ref, tail_ref, *, n_tok):
    n_seg = tail_ref.shape[0]
    x = x_ref[...]
    d = x.shape[1]
    h = _rmsnorm(x, gn_ref[...]).astype(_BF16)
    bg = _dot(h, win_ref[:, 0:d])
    ci = _dot(h, win_ref[:, d:2 * d]) * _dot(h, win_ref[:, 2 * d:3 * d]) + inj_ref[...]
    for s, rows in enumerate(_last_rows(ci, n_seg)):
        tail_ref[s] = rows
    y = _roll_conv(ci, cw_ref[...])
    out = x + _dot((bg * y).astype(_BF16), wout_ref[...])
    o_ref[...] = jnp.where(_token_rows(x.shape[0], d, n_tok), out, 0.0)


def _sc_short_call(x, inj, w, *, n_seg, n_tok):
    rows, d = x.shape
    args = (x, inj, w["gn"], w["win"], w["cw"], w["wout"])
    return pl.pallas_call(
        functools.partial(_sc_short_kernel, n_tok=n_tok),
        grid=(1,),
        in_specs=[_short_spec(a.shape) for a in args],
        out_specs=[_short_spec((rows, d)), _short_spec((n_seg, SUBLANES, d))],
        out_shape=[jax.ShapeDtypeStruct((rows, d), _F32),
                   jax.ShapeDtypeStruct((n_seg, SUBLANES, d), _F32)],
        compiler_params=_short_params(),
        name="shortconv_mixer_short",
    )(*args)


def _gla_short_kernel(x_ref, s0_ref, gn_ref, win_ref, wgd_ref, wgu_ref, bgate_ref, hg_ref, wout_ref,
                      o_ref, sout_ref, s_scr, p_scr, g_scr, o_scr, *, n_tok):
    n_seg = s0_ref.shape[0]
    c = SHORT_STRIDE
    hk = g_scr.shape[1]
    hv = o_scr.shape[1]
    x = x_ref[...]
    h = _rmsnorm(x, gn_ref[...]).astype(_BF16)
    p_scr[...] = _dot(h, win_ref[...])
    gdown = _dot(h, wgd_ref[...]).astype(_BF16)
    z = _dot(gdown, wgu_ref[...]) + bgate_ref[...]
    g = jnp.where(_token_rows(x.shape[0], hk, n_tok), _log_sigmoid(z) * (1.0 / GLA_TAU), 0.0)
    g_scr[...] = g
    sums = [jnp.sum(g[s * c:(s + 1) * c], axis=0, keepdims=True) for s in range(n_seg)]
    is_mild = jnp.min(functools.reduce(jnp.minimum, sums)) >= MILD_LOG_DECAY
    q_scale = float(hk // GLA_HEADS) ** -0.5
    masks = _level_masks(c)

    def walk(mild):
        def body(s, carry):
            rows = pl.ds(pl.multiple_of(s * c, c), c)
            q = p_scr[rows, 0:hk] * q_scale
            k = p_scr[rows, hk:2 * hk]
            v = p_scr[rows, 2 * hk:2 * hk + hv].astype(_BF16)
            s_scr[...] = s0_ref[s]
            pre = _gla_chunk_scores(q, k, g_scr[rows, :], c, masks, mild)
            o_scr[rows, :] = _gla_chunk_apply(pre, v, s_scr, hg_ref[...])
            sout_ref[s] = s_scr[...]
            return carry

        lax.fori_loop(0, n_seg, body, 0)

    @pl.when(is_mild)
    def _():
        walk(True)

    @pl.when(jnp.logical_not(is_mild))
    def _():
        walk(False)

    r = p_scr[:, 2 * hk + hv:2 * hk + 2 * hv]
    out = x + _dot((o_scr[...] * _silu(r)).astype(_BF16), wout_ref[...])
    o_ref[...] = jnp.where(_token_rows(x.shape[0], x.shape[1], n_tok), out, 0.0)


def _gla_short_call(x, s0, w, *, n_tok):
    rows, d = x.shape
    n_seg, n_h, dk, dv = s0.shape
    hk, hv = n_h * dk, n_h * dv
    args = (x, s0, w["gn"], w["win"], w["wgd"], w["wgu"], w["bgate"], w["hg"], w["wout"])
    return pl.pallas_call(
        functools.partial(_gla_short_kernel, n_tok=n_tok),
        grid=(1,),
        in_specs=[_short_spec(a.shape) for a in args],
        out_specs=[_short_spec((rows, d)), _short_spec(s0.shape)],
        out_shape=[jax.ShapeDtypeStruct((rows, d), _F32), jax.ShapeDtypeStruct(s0.shape, _F32)],
        scratch_shapes=[
            pltpu.VMEM((n_h, dk, dv), _F32),
            pltpu.VMEM((rows, 2 * hk + 2 * hv), _F32),
            pltpu.VMEM((rows, hk), _F32),
            pltpu.VMEM((rows, hv), _F32),
        ],
        compiler_params=_short_params(),
        name="gla_mixer_short",
    )(*args)


def _short_rows(a):
    n_seg, n_tok, width = a.shape
    return jnp.pad(a, ((0, 0), (SHORT_STRIDE - n_tok, 0), (0, 0))).reshape(n_seg * SHORT_STRIDE, width)


def _short_inject(cache, n_tok):
    n_seg, n_prev, width = cache.shape
    lead = SHORT_STRIDE - n_tok - n_prev
    return jnp.pad(cache, ((0, 0), (lead, n_tok), (0, 0))).reshape(n_seg * SHORT_STRIDE, width)


def _short_trunk(xs, gla_s, conv_c, ffn_c, params, gfin):
    n_seg, n_tok, d = xs.shape
    depth = len(params)
    x = _short_rows(xs)
    keep = lambda tail: tail[:, SUBLANES - (CONV_W - 1):, :]
    gla_new, conv_new, ffn_new = [], [], []
    for i, (mix, ffn) in enumerate(params):
        if i % 2 == 0:
            x, s = _gla_short_call(x, gla_s[i // 2], mix, n_tok=n_tok)
            gla_new.append(s)
        else:
            x, tail = _sc_short_call(x, _short_inject(conv_c[i // 2], n_tok), mix,
                                     n_seg=n_seg, n_tok=n_tok)
            conv_new.append(keep(tail))
        x, tail = _ffn_short_call(x, _short_inject(ffn_c[i], n_tok), ffn, gfin, n_seg=n_seg,
                                  n_tok=n_tok, final_norm=(i == depth - 1))
        ffn_new.append(keep(tail))
    y = x.reshape(n_seg, SHORT_STRIDE, d)[:, SHORT_STRIDE - n_tok:, :]
    return y, gla_new, conv_new, ffn_new


def _row(v):
    return v.reshape(1, -1).astype(_F32)


def _prep_gla(norm, w_in, w_gate_up, b_gate, head_gain, w_out):
    main = w_in.shape[1] - GLA_GATE_RANK
    wgd = jnp.pad(w_in[:, main:], ((0, 0), (0, LANES - GLA_GATE_RANK)))
    wgu = jnp.pad(w_gate_up, ((0, LANES - GLA_GATE_RANK), (0, 0)))
    return dict(gn=_row(norm), win=w_in[:, :main].astype(_BF16), wgd=wgd.astype(_BF16),
                wgu=wgu.astype(_BF16), bgate=_row(b_gate), hg=_row(head_gain),
                wout=w_out.astype(_BF16))


def _prep_sc(norm, w_in, conv_w, w_out):
    return dict(gn=_row(norm), win=w_in.astype(_BF16), cw=conv_w.astype(_F32),
                wout=w_out.astype(_BF16))


def _prep_ffn(norm, w_up, conv_w, conv_b, w_down):
    assert w_down.shape[0] % FFN_COLS == 0
    return dict(gn=_row(norm), wup=w_up.astype(_BF16), cw=conv_w.astype(_F32), cb=_row(conv_b),
                wd=w_down.astype(_BF16))


def _tail_from_cache(cache):
    return jnp.pad(cache, ((0, 0), (SUBLANES - (CONV_W - 1), 0), (0, 0)))


def _ffn_tail_from_cache(cache, n_c, fc):
    b = cache.shape[0]
    t = _tail_from_cache(cache).reshape(b, SUBLANES, 2, n_c, fc)
    return t.transpose(0, 2, 3, 1, 4)


def _ffn_cache_from_tail(tail):
    b = tail.shape[0]
    rows = tail[:, :, :, SUBLANES - (CONV_W - 1):, :]
    return rows.transpose(0, 3, 1, 2, 4).reshape(b, CONV_W - 1, -1)


def _trunk(x, gla_s, conv_tail, ffn_tail, params, gfin, *, tm_gla, tm_sc, tm_ffn, chunk):
    n_seq = x.shape[0]
    depth = len(params)
    gla_new, conv_new, ffn_new = [], [], []
    i_gla = i_conv = 0
    for i, (mix, ffn) in enumerate(params):
        if i % 2 == 0:
            x, s = _gla_call(x, gla_s[i_gla], mix, n_seq=n_seq, tm=tm_gla, chunk=chunk)
            gla_new.append(s)
            i_gla += 1
        else:
            x, s = _sc_call(x, conv_tail[i_conv], mix, n_seq=n_seq, tm=tm_sc)
            conv_new.append(s)
            i_conv += 1
        x, s = _ffn_call(x, ffn_tail[i], ffn, gfin, n_seq=n_seq, tm=tm_ffn,
                         final_norm=(i == depth - 1))
        ffn_new.append(s)
    return x, gla_new, conv_new, ffn_new


def kernel(x_prompt, x_sample, state_gla, cache_conv, cache_ffn, meta, norm_mix, norm_ffn, norm_final,
           gla_w_in, gla_w_gate_up, gla_b_gate, gla_head_gain, gla_w_out,
           sc_w_in, sc_conv_w, sc_w_out, ffn_w_up, ffn_conv_w, ffn_conv_b, ffn_w_down):
    depth = norm_mix.shape[0]
    n_b, seq, d = x_prompt.shape
    n_s, s_len, _ = x_sample.shape
    assert s_len == N_META and seq % GLA_CHUNK == 0
    f = ffn_w_down.shape[1]
    n_c, fc = f // FFN_COLS, FFN_COLS

    params = []
    for i in range(depth):
        j = i // 2
        if i % 2 == 0:
            mix = _prep_gla(norm_mix[i], gla_w_in[j], gla_w_gate_up[j], gla_b_gate[j],
                            gla_head_gain[j], gla_w_out[j])
        else:
            mix = _prep_sc(norm_mix[i], sc_w_in[j], sc_conv_w[j], sc_w_out[j])
        params.append((mix, _prep_ffn(norm_ffn[i], ffn_w_up[i], ffn_conv_w[i], ffn_conv_b[i],
                                      ffn_w_down[i])))
    gfin = _row(norm_final)

    xs = jnp.concatenate([x_sample, meta[None].astype(x_sample.dtype)], axis=0)
    zero1 = lambda a: jnp.zeros((a.shape[0], 1) + a.shape[2:], a.dtype)
    gla_s = jnp.concatenate([state_gla, zero1(state_gla)], axis=1).astype(_F32)
    conv_c = jnp.concatenate([cache_conv, zero1(cache_conv)], axis=1).astype(_F32)
    ffn_c = jnp.concatenate([cache_ffn, zero1(cache_ffn)], axis=1).astype(_F32)
    ys, gla_1, conv_1, ffn_1 = _short_trunk(xs, gla_s, conv_c, ffn_c, params, gfin)

    yp, gla_2, conv_2, ffn_2 = _trunk(
        x_prompt, [s[n_s:] for s in gla_1],
        [_tail_from_cache(s[n_s:]) for s in conv_1],
        [_ffn_tail_from_cache(s[n_s:], n_c, fc) for s in ffn_1],
        params, gfin, tm_gla=min(TILE_GLA, seq), tm_sc=min(TILE_SC, seq), tm_ffn=min(TILE_FFN, seq),
        chunk=GLA_CHUNK)

    dt = x_prompt.dtype
    tail2 = lambda s: s[:, SUBLANES - (CONV_W - 1):, :]
    return (yp.astype(dt), ys[:n_s].astype(dt),
            jnp.stack(gla_2).astype(dt), jnp.stack([s[:n_s] for s in gla_1]).astype(dt),
            jnp.stack([tail2(s) for s in conv_2]).astype(dt),
            jnp.stack([s[:n_s] for s in conv_1]).astype(dt),
            jnp.stack([_ffn_cache_from_tail(s) for s in ffn_2]).astype(dt),
            jnp.stack([s[:n_s] for s in ffn_1]).astype(dt))
```

```python
import functools

import jax
import jax.numpy as jnp
from jax import lax
from jax.experimental import pallas as pl
from jax.experimental.pallas import tpu as pltpu

N_META = 16
GLA_HEADS = 4
GLA_GATE_RANK = 16
GLA_TAU = 16.0
CONV_W = 3
EPS = 1e-6
GLA_CHUNK = 256
MILD_LOG_DECAY = -60.0
FFN_COLS = 256
SC_COLS = 256
TILE_GLA = 512
TILE_SC = 512
TILE_FFN = 512
LANES = 128
SUBLANES = 8
VMEM_LIMIT_BYTES = 56 * 1024 * 1024

_BF16 = jnp.bfloat16
_F32 = jnp.float32
_LOG2E = 1.4426950408889634


def _dot(a, b):
    return lax.dot_general(a, b, (((1,), (0,)), ((), ())), preferred_element_type=_F32)


def _dot_nt(a, b):
    return lax.dot_general(a, b, (((1,), (1,)), ((), ())), preferred_element_type=_F32)


def _dot_tn(a, b):
    return lax.dot_general(a, b, (((0,), (0,)), ((), ())), preferred_element_type=_F32)


def _rmsnorm(x, g):
    return x * lax.rsqrt(jnp.mean(x * x, axis=-1, keepdims=True) + EPS) * g


def _silu(x):
    return x / (1.0 + jnp.exp2(x * (-_LOG2E)))


def _log_sigmoid(z):
    return jnp.minimum(z, 0.0) - jnp.log(1.0 + jnp.exp(-jnp.abs(z)))


def _shift_rows(z, tail, k):
    rolled = pltpu.roll(z, k, 0)
    head = rolled[:SUBLANES]
    rid = lax.broadcasted_iota(jnp.int32, head.shape, 0)
    head = jnp.where(rid < k, pltpu.roll(tail, k, 0), head)
    if z.shape[0] == SUBLANES:
        return head
    return jnp.concatenate([head, rolled[SUBLANES:]], axis=0)


def _causal_conv(z, tail, w):
    return w[2:3] * z + w[1:2] * _shift_rows(z, tail, 1) + w[0:1] * _shift_rows(z, tail, 2)


def _const_spec(shape):
    nd = len(shape)
    return pl.BlockSpec(shape, lambda b, t: (0,) * nd, pipeline_mode=pl.Buffered(1))


def _init_spec(shape, shared):
    nd = len(shape)
    block = (1,) + tuple(shape[1:])
    if shared:
        return pl.BlockSpec(block, lambda b, t: (0,) * nd)
    return pl.BlockSpec(block, lambda b, t: (b,) + (0,) * (nd - 1))


def _tile_spec(tm, d):
    return pl.BlockSpec((1, tm, d), lambda b, t: (b, t, 0))


def _next_tile_spec(tm, d, n_t):
    return pl.BlockSpec((1, tm, d), lambda b, t: (b, jnp.minimum(t + 1, n_t - 1), 0))


def _compiler_params(flags=None):
    return pltpu.CompilerParams(dimension_semantics=("arbitrary", "arbitrary"),
                                vmem_limit_bytes=VMEM_LIMIT_BYTES, flags=flags)


def _ffn_kernel(x_ref, init_ref, gn_ref, wup_ref, cw_ref, cb_ref, wd_ref,
                gfin_ref, o_ref, cache_ref, *z_bufs, final_norm):
    t = pl.program_id(1)
    n_t = pl.num_programs(1)
    n_c = len(z_bufs) // 2
    zg_bufs, zu_bufs = z_bufs[:n_c], z_bufs[n_c:]
    fc = zg_bufs[0].shape[1]
    f = n_c * fc
    tm = x_ref.shape[1]
    s8 = SUBLANES

    @pl.when(t == 0)
    def _():
        for j in range(n_c):
            zg_bufs[j][0:s8, :] = init_ref[0, 0, j]
            zu_bufs[j][0:s8, :] = init_ref[0, 1, j]

    x = x_ref[0]
    h = _rmsnorm(x, gn_ref[...]).astype(_BF16)

    def up(j):
        zg_bufs[j][s8:s8 + tm, :] = _dot(h, wup_ref[:, j * fc:(j + 1) * fc])
        zu_bufs[j][s8:s8 + tm, :] = _dot(h, wup_ref[:, f + j * fc:f + (j + 1) * fc])

    sub = lax.broadcasted_iota(jnp.int32, (1, s8, fc), 1)

    def conv(buf, c0):
        w = cw_ref[:, c0:c0 + fc]
        cur = buf[s8:s8 + tm, :].reshape(tm // s8, s8, fc)
        head = buf[0:s8, :].reshape(1, s8, fc)
        shifted = []
        for k in (1, 2):
            rot = pltpu.roll(cur, k, 1)
            prev = jnp.concatenate([pltpu.roll(head, k, 1), rot[:-1]], axis=0)
            shifted.append(jnp.where(sub < k, prev, rot))
        y = w[2:3] * cur + w[1:2] * shifted[0] + w[0:1] * shifted[1] + cb_ref[:, c0:c0 + fc]
        return y.reshape(tm, fc)

    def act(j):
        a = _silu(conv(zg_bufs[j], j * fc)) * conv(zu_bufs[j], f + j * fc)
        zg_bufs[j][0:s8, :] = zg_bufs[j][tm:tm + s8, :]
        zu_bufs[j][0:s8, :] = zu_bufs[j][tm:tm + s8, :]
        return a.astype(_BF16)

    for j in range(n_c):
        up(j)
    acts = [act(j) for j in range(n_c)]
    acc = x
    for j in range(n_c):
        acc = acc + _dot(acts[j], wd_ref[j * fc:(j + 1) * fc, :])
    if final_norm:
        acc = _rmsnorm(acc, gfin_ref[...])
    o_ref[0] = acc

    @pl.when(t == n_t - 1)
    def _():
        for j in range(n_c):
            cache_ref[0, 0, j] = zg_bufs[j][0:s8, :]
            cache_ref[0, 1, j] = zu_bufs[j][0:s8, :]


def _ffn_call(x, init, w, gfin, *, n_seq, tm, final_norm):
    _, seq, d = x.shape
    fc = FFN_COLS
    n_c = w["wd"].shape[0] // fc
    shared = init.shape[0] == 1 and n_seq > 1
    tail_shape = (n_c, SUBLANES, fc)
    out, cache = pl.pallas_call(
        functools.partial(_ffn_kernel, final_norm=final_norm),
        grid=(n_seq, seq // tm),
        in_specs=[
            _tile_spec(tm, d),
            _init_spec(init.shape, shared),
            _const_spec(w["gn"].shape),
            _const_spec(w["wup"].shape), _const_spec(w["cw"].shape), _const_spec(w["cb"].shape),
            _const_spec(w["wd"].shape),
            _const_spec(gfin.shape),
        ],
        out_specs=[
            _tile_spec(tm, d),
            pl.BlockSpec((1, 2) + tail_shape, lambda b, t: (b, 0, 0, 0, 0)),
        ],
        out_shape=[
            jax.ShapeDtypeStruct((n_seq, seq, d), _F32),
            jax.ShapeDtypeStruct((n_seq, 2) + tail_shape, _F32),
        ],
        scratch_shapes=[pltpu.VMEM((SUBLANES + tm, fc), _F32) for _ in range(2 * n_c)],
        compiler_params=_compiler_params(),
        name="conv_ffn",
    )(x, init, w["gn"], w["wup"], w["cw"], w["cb"], w["wd"], gfin)
    return out, cache


def _sc_kernel(x_ref, init_ref, gn_ref, win_ref, cw_ref, wout_ref, o_ref, cache_ref, tail):
    t = pl.program_id(1)
    n_t = pl.num_programs(1)
    tm, d = x_ref.shape[1], x_ref.shape[2]
    wc = SC_COLS

    @pl.when(t == 0)
    def _():
        tail[...] = init_ref[0]

    x = x_ref[0]
    h = _rmsnorm(x, gn_ref[...]).astype(_BF16)
    blocks = [slice(c0, c0 + wc) for c0 in range(0, d, wc)]
    proj = [[_dot(h, win_ref[:, i * d + c.start:i * d + c.stop]) for i in range(3)] for c in blocks]
    gated = []
    for c, (p_b, p_c, p_u) in zip(blocks, proj):
        ci = p_c * p_u
        y = _causal_conv(ci, tail[:, c], cw_ref[:, c])
        tail[:, c] = ci[tm - SUBLANES:]
        gated.append((p_b * y).astype(_BF16))
    acc = x
    for c, a in zip(blocks, gated):
        acc = acc + _dot(a, wout_ref[c, :])
    o_ref[0] = acc

    @pl.when(t == n_t - 1)
    def _():
        cache_ref[0] = tail[...]


def _sc_call(x, init, w, *, n_seq, tm):
    _, seq, d = x.shape
    shared = init.shape[0] == 1 and n_seq > 1
    out, cache = pl.pallas_call(
        _sc_kernel,
        grid=(n_seq, seq // tm),
        in_specs=[
            _tile_spec(tm, d),
            _init_spec(init.shape, shared),
            _const_spec(w["gn"].shape), _const_spec(w["win"].shape),
            _const_spec(w["cw"].shape), _const_spec(w["wout"].shape),
        ],
        out_specs=[
            _tile_spec(tm, d),
            pl.BlockSpec((1, SUBLANES, d), lambda b, t: (b, 0, 0)),
        ],
        out_shape=[
            jax.ShapeDtypeStruct((n_seq, seq, d), _F32),
            jax.ShapeDtypeStruct((n_seq, SUBLANES, d), _F32),
        ],
        scratch_shapes=[pltpu.VMEM((SUBLANES, d), _F32)],
        compiler_params=_compiler_params(),
        name="shortconv_mixer",
    )(x, init, w["gn"], w["win"], w["cw"], w["wout"])
    return out, cache


def _cumsum_rows(a):
    n = a.shape[0]
    rid = lax.broadcasted_iota(jnp.int32, (SUBLANES, a.shape[1]), 0)
    groups = []
    for i in range(0, n, SUBLANES):
        grp = a[i:i + SUBLANES]
        s = 1
        while s < SUBLANES:
            grp = grp + jnp.where(rid >= s, pltpu.roll(grp, s, 0), 0.0)
            s *= 2
        if groups:
            grp = grp + groups[-1][SUBLANES - 1:SUBLANES]
        groups.append(grp)
    return jnp.concatenate(groups, axis=0) if len(groups) > 1 else groups[0]


def _level_masks(c):
    tt = lax.broadcasted_iota(jnp.int32, (c, c), 0)
    ss = lax.broadcasted_iota(jnp.int32, (c, c), 1)
    txs = tt ^ ss
    lower = tt > ss
    masks = [tt == ss]
    half = 1
    while half < c:
        masks.append(lower & (txs >= half) & (txs < 2 * half))
        half *= 2
    return masks


def _gla_chunk_scores(q, k, g, c, masks, mild):
    n_h = GLA_HEADS
    dk = q.shape[1] // n_h
    b = _cumsum_rows(g)
    b_last = b[c - 1:c]
    qe = (q * jnp.exp(b)).astype(_BF16)
    kd = (k * jnp.exp(b_last - b)).astype(_BF16)
    decay = jnp.exp(b_last)

    if mild:
        k_inv = (k * jnp.exp(-b)).astype(_BF16)
        tt = lax.broadcasted_iota(jnp.int32, (c, c), 0)
        causal = tt >= lax.broadcasted_iota(jnp.int32, (c, c), 1)
        scores = []
        for hh in range(n_h):
            ks = slice(hh * dk, (hh + 1) * dk)
            a = jnp.where(causal, _dot_nt(qe[:, ks], k_inv[:, ks]), 0.0)
            scores.append(a.astype(_BF16))
        return dict(scores=scores, qe=qe, kd=kd, decay=decay)

    rid = lax.broadcasted_iota(jnp.int32, b.shape, 0)
    levels = [(q.astype(_BF16), k.astype(_BF16))]
    e_k = b
    half = 1
    while half < c:
        hi = (rid & half) != 0
        b_mid = jnp.where(hi, pltpu.roll(e_k, half, 0), e_k)
        scale = jnp.exp2((b - b_mid) * jnp.where(hi, _LOG2E, -_LOG2E))
        x = (jnp.where(hi, q, k) * scale).astype(_BF16)
        levels.append((x, x))
        if 2 * half < c:
            e_k = jnp.where(hi, e_k, pltpu.roll(e_k, c - half, 0))
        half *= 2

    scores = []
    for hh in range(n_h):
        ks = slice(hh * dk, (hh + 1) * dk)
        a = jnp.zeros((c, c), _F32)
        for mask, (qs, kk) in zip(masks, levels):
            a = jnp.where(mask, _dot_nt(qs[:, ks], kk[:, ks]), a)
        scores.append(a.astype(_BF16))
    return dict(scores=scores, qe=qe, kd=kd, decay=decay)


def _gla_chunk_apply(pre, v, s_ref, hg):
    n_h = GLA_HEADS
    qe, kd, decay = pre["qe"], pre["kd"], pre["decay"]
    dk = qe.shape[1] // n_h
    dv = v.shape[1] // n_h
    outs = []
    for hh in range(n_h):
        ks = slice(hh * dk, (hh + 1) * dk)
        vs = slice(hh * dv, (hh + 1) * dv)
        s_old = s_ref[hh]
        o = _dot(qe[:, ks], s_old.astype(_BF16)) + _dot(pre["scores"][hh], v[:, vs])
        o = o * lax.rsqrt(jnp.mean(o * o, axis=-1, keepdims=True) + EPS) * hg[:, vs]
        outs.append(o)
        dcol = jnp.transpose(jnp.broadcast_to(decay[:, ks], (dk, dk)))
        dmat = jnp.concatenate([dcol] * (dv // dk), axis=1)
        s_ref[hh] = dmat * s_old + _dot_tn(kd[:, ks], v[:, vs])
    return jnp.concatenate(outs, axis=1)


def _gla_kernel(x_ref, s0_ref, gn_ref, win_ref, wgd_ref, wgu_ref, bgate_ref, hg_ref, wout_ref,
                o_ref, sout_ref, s_scr, p_scr, g_scr, o_scr, h_scr, *, chunk):
    t = pl.program_id(1)
    n_t = pl.num_programs(1)
    tm = x_ref.shape[1]
    hk = g_scr.shape[1]
    hv = o_scr.shape[1]
    n_chunks = tm // chunk

    @pl.when(t == 0)
    def _():
        s_scr[...] = s0_ref[0]

    h_scr[...] = _rmsnorm(x_ref[0], gn_ref[...]).astype(_BF16)
    h = h_scr[...]
    gdown = _dot(h, wgd_ref[...]).astype(_BF16)
    z = _dot(gdown, wgu_ref[...]) + bgate_ref[...]
    g = _log_sigmoid(z) * (1.0 / GLA_TAU)
    g_scr[...] = g
    p_scr[:, 0:2 * hk] = _dot(h, win_ref[:, 0:2 * hk])
    chunk_sums = [jnp.sum(g[ci * chunk:(ci + 1) * chunk], axis=0, keepdims=True)
                  for ci in range(n_chunks)]
    lowest = jnp.min(functools.reduce(jnp.minimum, chunk_sums))
    q_scale = float(hk // GLA_HEADS) ** -0.5

    def step_body(mild):
        masks = _level_masks(chunk)
        hb = h_scr[...]
        half = hv // 2
        pieces = [(2 * hk + i * half, 2 * hk + (i + 1) * half) for i in range(4)]

        def project(i):
            c0, c1 = pieces[i]
            p_scr[:, c0:c1] = _dot(hb, win_ref[:, c0:c1])

        def scores(ci):
            rows = slice(ci * chunk, (ci + 1) * chunk)
            q = p_scr[rows, 0:hk] * q_scale
            k = p_scr[rows, hk:2 * hk]
            return _gla_chunk_scores(q, k, g_scr[rows, :], chunk, masks, mild)

        def apply(ci, pre):
            rows = slice(ci * chunk, (ci + 1) * chunk)
            v = p_scr[rows, 2 * hk:2 * hk + hv].astype(_BF16)
            o_scr[rows, :] = _gla_chunk_apply(pre, v, s_scr, hg_ref[...])

        project(0)
        pre = scores(0)
        project(1)
        for ci in range(n_chunks):
            nxt_pre = scores(ci + 1) if ci + 1 < n_chunks else None
            apply(ci, pre)
            if ci < 2:
                project(2 + ci)
            pre = nxt_pre
        for i in range(2 + min(2, n_chunks), 4):
            project(i)
        r = p_scr[:, 2 * hk + hv:2 * hk + 2 * hv]
        o_ref[0] = x_ref[0] + _dot((o_scr[...] * _silu(r)).astype(_BF16), wout_ref[...])

    is_mild = lowest >= MILD_LOG_DECAY

    @pl.when(is_mild)
    def _():
        step_body(True)

    @pl.when(jnp.logical_not(is_mild))
    def _():
        step_body(False)

    @pl.when(t == n_t - 1)
    def _():
        sout_ref[0] = s_scr[...]


def _gla_call(x, s0, w, *, n_seq, tm, chunk):
    _, seq, d = x.shape
    n_h, dk, dv = s0.shape[1:]
    hk, hv = n_h * dk, n_h * dv
    shared = s0.shape[0] == 1 and n_seq > 1
    out, s_out = pl.pallas_call(
        functools.partial(_gla_kernel, chunk=chunk),
        grid=(n_seq, seq // tm),
        in_specs=[
            _tile_spec(tm, d),
            _init_spec(s0.shape, shared),
            _const_spec(w["gn"].shape), _const_spec(w["win"].shape),
            _const_spec(w["wgd"].shape), _const_spec(w["wgu"].shape),
            _const_spec(w["bgate"].shape), _const_spec(w["hg"].shape),
            _const_spec(w["wout"].shape),
        ],
        out_specs=[
            _tile_spec(tm, d),
            pl.BlockSpec((1, n_h, dk, dv), lambda b, t: (b, 0, 0, 0)),
        ],
        out_shape=[
            jax.ShapeDtypeStruct((n_seq, seq, d), _F32),
            jax.ShapeDtypeStruct((n_seq, n_h, dk, dv), _F32),
        ],
        scratch_shapes=[
            pltpu.VMEM((n_h, dk, dv), _F32),
            pltpu.VMEM((tm, 2 * hk + 2 * hv), _F32),
            pltpu.VMEM((tm, hk), _F32),
            pltpu.VMEM((tm, hv), _F32),
            pltpu.VMEM((tm, d), _BF16),
        ],
        compiler_params=_compiler_params(),
        name="gla_mixer",
    )(x, s0, w["gn"], w["win"], w["wgd"], w["wgu"], w["bgate"], w["hg"], w["wout"])
    return out, s_out


SHORT_STRIDE = 32


def _short_spec(shape):
    nd = len(shape)
    return pl.BlockSpec(shape, lambda i: (0,) * nd, pipeline_mode=pl.Buffered(1))


def _short_params():
    return pltpu.CompilerParams(dimension_semantics=("arbitrary",), vmem_limit_bytes=VMEM_LIMIT_BYTES)


def _token_rows(n_rows, width, n_tok):
    pos = lax.broadcasted_iota(jnp.int32, (n_rows, width), 0) & (SHORT_STRIDE - 1)
    return pos >= SHORT_STRIDE - n_tok


def _roll_conv(z, w):
    return w[2:3] * z + w[1:2] * pltpu.roll(z, 1, 0) + w[0:1] * pltpu.roll(z, 2, 0)


def _last_rows(z, n_seg):
    return [z[(s + 1) * SHORT_STRIDE - SUBLANES:(s + 1) * SHORT_STRIDE] for s in range(n_seg)]


def _ffn_short_kernel(x_ref, inj_ref, gn_ref, wup_ref, cw_ref, cb_ref, wd_ref, gfin_ref,
                      o_ref, tail_ref, *, final_norm, n_tok):
    n_seg = tail_ref.shape[0]
    f = wd_ref.shape[0]
    fc = FFN_COLS
    x = x_ref[...]
    h = _rmsnorm(x, gn_ref[...]).astype(_BF16)
    acc = x
    for j in range(f // fc):
        halves = []
        for c0 in (j * fc, f + j * fc):
            cols = slice(c0, c0 + fc)
            z = _dot(h, wup_ref[:, cols]) + inj_ref[:, cols]
            for s, rows in enumerate(_last_rows(z, n_seg)):
                tail_ref[s, :, cols] = rows
            halves.append(_roll_conv(z, cw_ref[:, cols]) + cb_ref[:, cols])
        a = (_silu(halves[0]) * halves[1]).astype(_BF16)
        acc = acc + _dot(a, wd_ref[j * fc:(j + 1) * fc, :])
    if final_norm:
        acc = _rmsnorm(acc, gfin_ref[...])
    o_ref[...] = jnp.where(_token_rows(x.shape[0], x.shape[1], n_tok), acc, 0.0)


def _ffn_short_call(x, inj, w, gfin, *, n_seg, n_tok, final_norm):
    rows, d = x.shape
    f2 = w["wup"].shape[1]
    args = (x, inj, w["gn"], w["wup"], w["cw"], w["cb"], w["wd"], gfin)
    return pl.pallas_call(
        functools.partial(_ffn_short_kernel, final_norm=final_norm, n_tok=n_tok),
        grid=(1,),
        in_specs=[_short_spec(a.shape) for a in args],
        out_specs=[_short_spec((rows, d)), _short_spec((n_seg, SUBLANES, f2))],
        out_shape=[jax.ShapeDtypeStruct((rows, d), _F32),
                   jax.ShapeDtypeStruct((n_seg, SUBLANES, f2), _F32)],
        compiler_params=_short_params(),
        name="conv_ffn_short",
    )(*args)


def _sc_short_kernel(x_ref, inj_ref, gn_ref, win_ref, cw_ref, wout_ref, o_ref, tail_ref, *, n_tok):
    n_seg = tail_ref.shape[0]
    x = x_ref[...]
    d = x.shape[1]
    h = _rmsnorm(x, gn_ref[...]).astype(_BF16)
    bg = _dot(h, win_ref[:, 0:d])
    ci = _dot(h, win_ref[:, d:2 * d]) * _dot(h, win_ref[:, 2 * d:3 * d]) + inj_ref[...]
    for s, rows in enumerate(_last_rows(ci, n_seg)):
        tail_ref[s] = rows
    y = _roll_conv(ci, cw_ref[...])
    out = x + _dot((bg * y).astype(_BF16), wout_ref[...])
    o_ref[...] = jnp.where(_token_rows(x.shape[0], d, n_tok), out, 0.0)


def _sc_short_call(x, inj, w, *, n_seg, n_tok):
    rows, d = x.shape
    args = (x, inj, w["gn"], w["win"], w["cw"], w["wout"])
    return pl.pallas_call(
        functools.partial(_sc_short_kernel, n_tok=n_tok),
        grid=(1,),
        in_specs=[_short_spec(a.shape) for a in args],
        out_specs=[_short_spec((rows, d)), _short_spec((n_seg, SUBLANES, d))],
        out_shape=[jax.ShapeDtypeStruct((rows, d), _F32),
                   jax.ShapeDtypeStruct((n_seg, SUBLANES, d), _F32)],
        compiler_params=_short_params(),
        name="shortconv_mixer_short",
    )(*args)


def _gla_short_kernel(x_ref, s0_ref, gn_ref, win_ref, wgd_ref, wgu_ref, bgate_ref, hg_ref, wout_ref,
                      o_ref, sout_ref, s_scr, p_scr, g_scr, o_scr, *, n_tok):
    n_seg = s0_ref.shape[0]
    c = SHORT_STRIDE
    hk = g_scr.shape[1]
    hv = o_scr.shape[1]
    x = x_ref[...]
    h = _rmsnorm(x, gn_ref[...]).astype(_BF16)
    p_scr[...] = _dot(h, win_ref[...])
    gdown = _dot(h, wgd_ref[...]).astype(_BF16)
    z = _dot(gdown, wgu_ref[...]) + bgate_ref[...]
    g = jnp.where(_token_rows(x.shape[0], hk, n_tok), _log_sigmoid(z) * (1.0 / GLA_TAU), 0.0)
    g_scr[...] = g
    sums = [jnp.sum(g[s * c:(s + 1) * c], axis=0, keepdims=True) for s in range(n_seg)]
    is_mild = jnp.min(functools.reduce(jnp.minimum, sums)) >= MILD_LOG_DECAY
    q_scale = float(hk // GLA_HEADS) ** -0.5
    masks = _level_masks(c)

    def walk(mild):
        def body(s, carry):
            rows = pl.ds(pl.multiple_of(s * c, c), c)
            q = p_scr[rows, 0:hk] * q_scale
            k = p_scr[rows, hk:2 * hk]
            v = p_scr[rows, 2 * hk:2 * hk + hv].astype(_BF16)
            s_scr[...] = s0_ref[s]
            pre = _gla_chunk_scores(q, k, g_scr[rows, :], c, masks, mild)
            o_scr[rows, :] = _gla_chunk_apply(pre, v, s_scr, hg_ref[...])
            sout_ref[s] = s_scr[...]
            return carry

        lax.fori_loop(0, n_seg, body, 0)

    @pl.when(is_mild)
    def _():
        walk(True)

    @pl.when(jnp.logical_not(is_mild))
    def _():
        walk(False)

    r = p_scr[:, 2 * hk + hv:2 * hk + 2 * hv]
    out = x + _dot((o_scr[...] * _silu(r)).astype(_BF16), wout_ref[...])
    o_ref[...] = jnp.where(_token_rows(x.shape[0], x.shape[1], n_tok), out, 0.0)


def _gla_short_call(x, s0, w, *, n_tok):
    rows, d = x.shape
    n_seg, n_h, dk, dv = s0.shape
    hk, hv = n_h * dk, n_h * dv
    args = (x, s0, w["gn"], w["win"], w["wgd"], w["wgu"], w["bgate"], w["hg"], w["wout"])
    return pl.pallas_call(
        functools.partial(_gla_short_kernel, n_tok=n_tok),
        grid=(1,),
        in_specs=[_short_spec(a.shape) for a in args],
        out_specs=[_short_spec((rows, d)), _short_spec(s0.shape)],
        out_shape=[jax.ShapeDtypeStruct((rows, d), _F32), jax.ShapeDtypeStruct(s0.shape, _F32)],
        scratch_shapes=[
            pltpu.VMEM((n_h, dk, dv), _F32),
            pltpu.VMEM((rows, 2 * hk + 2 * hv), _F32),
            pltpu.VMEM((rows, hk), _F32),
            pltpu.VMEM((rows, hv), _F32),
        ],
        compiler_params=_short_params(),
        name="gla_mixer_short",
    )(*args)


def _short_rows(a):
    n_seg, n_tok, width = a.shape
    return jnp.pad(a, ((0, 0), (SHORT_STRIDE - n_tok, 0), (0, 0))).reshape(n_seg * SHORT_STRIDE, width)


def _short_inject(cache, n_tok):
    n_seg, n_prev, width = cache.shape
    lead = SHORT_STRIDE - n_tok - n_prev
    return jnp.pad(cache, ((0, 0), (lead, n_tok), (0, 0))).reshape(n_seg * SHORT_STRIDE, width)


def _short_trunk(xs, gla_s, conv_c, ffn_c, params, gfin):
    n_seg, n_tok, d = xs.shape
    depth = len(params)
    x = _short_rows(xs)
    keep = lambda tail: tail[:, SUBLANES - (CONV_W - 1):, :]
    gla_new, conv_new, ffn_new = [], [], []
    for i, (mix, ffn) in enumerate(params):
        if i % 2 == 0:
            x, s = _gla_short_call(x, gla_s[i // 2], mix, n_tok=n_tok)
            gla_new.append(s)
        else:
            x, tail = _sc_short_call(x, _short_inject(conv_c[i // 2], n_tok), mix,
                                     n_seg=n_seg, n_tok=n_tok)
            conv_new.append(keep(tail))
        x, tail = _ffn_short_call(x, _short_inject(ffn_c[i], n_tok), ffn, gfin, n_seg=n_seg,
                                  n_tok=n_tok, final_norm=(i == depth - 1))
        ffn_new.append(keep(tail))
    y = x.reshape(n_seg, SHORT_STRIDE, d)[:, SHORT_STRIDE - n_tok:, :]
    return y, gla_new, conv_new, ffn_new


def _row(v):
    return v.reshape(1, -1).astype(_F32)


def _prep_gla(norm, w_in, w_gate_up, b_gate, head_gain, w_out):
    main = w_in.shape[1] - GLA_GATE_RANK
    wgd = jnp.pad(w_in[:, main:], ((0, 0), (0, LANES - GLA_GATE_RANK)))
    wgu = jnp.pad(w_gate_up, ((0, LANES - GLA_GATE_RANK), (0, 0)))
    return dict(gn=_row(norm), win=w_in[:, :main].astype(_BF16), wgd=wgd.astype(_BF16),
                wgu=wgu.astype(_BF16), bgate=_row(b_gate), hg=_row(head_gain),
                wout=w_out.astype(_BF16))


def _prep_sc(norm, w_in, conv_w, w_out):
    return dict(gn=_row(norm), win=w_in.astype(_BF16), cw=conv_w.astype(_F32),
                wout=w_out.astype(_BF16))


def _prep_ffn(norm, w_up, conv_w, conv_b, w_down):
    assert w_down.shape[0] % FFN_COLS == 0
    return dict(gn=_row(norm), wup=w_up.astype(_BF16), cw=conv_w.astype(_F32), cb=_row(conv_b),
                wd=w_down.astype(_BF16))


def _tail_from_cache(cache):
    return jnp.pad(cache, ((0, 0), (SUBLANES - (CONV_W - 1), 0), (0, 0)))


def _ffn_tail_from_cache(cache, n_c, fc):
    b = cache.shape[0]
    t = _tail_from_cache(cache).reshape(b, SUBLANES, 2, n_c, fc)
    return t.transpose(0, 2, 3, 1, 4)


def _ffn_cache_from_tail(tail):
    b = tail.shape[0]
    rows = tail[:, :, :, SUBLANES - (CONV_W - 1):, :]
    return rows.transpose(0, 3, 1, 2, 4).reshape(b, CONV_W - 1, -1)


def _trunk(x, gla_s, conv_tail, ffn_tail, params, gfin, *, tm_gla, tm_sc, tm_ffn, chunk):
    n_seq = x.shape[0]
    depth = len(params)
    gla_new, conv_new, ffn_new = [], [], []
    i_gla = i_conv = 0
    for i, (mix, ffn) in enumerate(params):
        if i % 2 == 0:
            x, s = _gla_call(x, gla_s[i_gla], mix, n_seq=n_seq, tm=tm_gla, chunk=chunk)
            gla_new.append(s)
            i_gla += 1
        else:
            x, s = _sc_call(x, conv_tail[i_conv], mix, n_seq=n_seq, tm=tm_sc)
            conv_new.append(s)
            i_conv += 1
        x, s = _ffn_call(x, ffn_tail[i], ffn, gfin, n_seq=n_seq, tm=tm_ffn,
                         final_norm=(i == depth - 1))
        ffn_new.append(s)
    return x, gla_new, conv_new, ffn_new


def kernel(x_prompt, x_sample, state_gla, cache_conv, cache_ffn, meta, norm_mix, norm_ffn, norm_final,
           gla_w_in, gla_w_gate_up, gla_b_gate, gla_head_gain, gla_w_out,
           sc_w_in, sc_conv_w, sc_w_out, ffn_w_up, ffn_conv_w, ffn_conv_b, ffn_w_down):
    depth = norm_mix.shape[0]
    n_b, seq, d = x_prompt.shape
    n_s, s_len, _ = x_sample.shape
    assert s_len == N_META and seq % GLA_CHUNK == 0
    f = ffn_w_down.shape[1]
    n_c, fc = f // FFN_COLS, FFN_COLS

    params = []
    for i in range(depth):
        j = i // 2
        if i % 2 == 0:
            mix = _prep_gla(norm_mix[i], gla_w_in[j], gla_w_gate_up[j], gla_b_gate[j],
                            gla_head_gain[j], gla_w_out[j])
        else:
            mix = _prep_sc(norm_mix[i], sc_w_in[j], sc_conv_w[j], sc_w_out[j])
        params.append((mix, _prep_ffn(norm_ffn[i], ffn_w_up[i], ffn_conv_w[i], ffn_conv_b[i],
                                      ffn_w_down[i])))
    gfin = _row(norm_final)

    xs = jnp.concatenate([x_sample, meta[None].astype(x_sample.dtype)], axis=0)
    zero1 = lambda a: jnp.zeros((a.shape[0], 1) + a.shape[2:], a.dtype)
    gla_s = jnp.concatenate([state_gla, zero1(state_gla)], axis=1).astype(_F32)
    conv_c = jnp.concatenate([cache_conv, zero1(cache_conv)], axis=1).astype(_F32)
    ffn_c = jnp.concatenate([cache_ffn, zero1(cache_ffn)], axis=1).astype(_F32)
    ys, gla_1, conv_1, ffn_1 = _short_trunk(xs, gla_s, conv_c, ffn_c, params, gfin)

    yp, gla_2, conv_2, ffn_2 = _trunk(
        x_prompt, [s[n_s:] for s in gla_1],
        [_tail_from_cache(s[n_s:]) for s in conv_1],
        [_ffn_tail_from_cache(s[n_s:], n_c, fc) for s in ffn_1],
        params, gfin, tm_gla=min(TILE_GLA, seq), tm_sc=min(TILE_SC, seq), tm_ffn=min(TILE_FFN, seq),
        chunk=GLA_CHUNK)

    dt = x_prompt.dtype
    tail2 = lambda s: s[:, SUBLANES - (CONV_W - 1):, :]
    return (yp.astype(dt), ys[:n_s].astype(dt),
            jnp.stack(gla_2).astype(dt), jnp.stack([s[:n_s] for s in gla_1]).astype(dt),
            jnp.stack([tail2(s) for s in conv_2]).astype(dt),
            jnp.stack([s[:n_s] for s in conv_1]).astype(dt),
            jnp.stack([_ffn_cache_from_tail(s) for s in ffn_2]).astype(dt),
            jnp.stack([s[:n_s] for s in ffn_1]).astype(dt))
```

```python
import functools

import jax
import jax.numpy as jnp
from jax import lax
from jax.experimental import pallas as pl
from jax.experimental.pallas import tpu as pltpu

N_META = 16
GLA_HEADS = 4
GLA_GATE_RANK = 16
GLA_TAU = 16.0
CONV_W = 3
EPS = 1e-6
GLA_CHUNK = 256
MILD_LOG_DECAY = -60.0
FFN_COLS = 256
SC_COLS = 256
FFN_DOWN_ROWS = 256
TILE_GLA = 512
TILE_SC = 512
TILE_FFN = 512
LANES = 128
SUBLANES = 8
VMEM_LIMIT_BYTES = 56 * 1024 * 1024

_BF16 = jnp.bfloat16
_F32 = jnp.float32
_LOG2E = 1.4426950408889634


def _dot(a, b):
    return lax.dot_general(a, b, (((1,), (0,)), ((), ())), preferred_element_type=_F32)


def _dot_nt(a, b):
    return lax.dot_general(a, b, (((1,), (1,)), ((), ())), preferred_element_type=_F32)


def _dot_tn(a, b):
    return lax.dot_general(a, b, (((0,), (0,)), ((), ())), preferred_element_type=_F32)


def _rmsnorm(x, g):
    return x * lax.rsqrt(jnp.mean(x * x, axis=-1, keepdims=True) + EPS) * g


def _silu(x):
    return x / (1.0 + jnp.exp2(x * (-_LOG2E)))


def _log_sigmoid(z):
    return jnp.minimum(z, 0.0) - jnp.log(1.0 + jnp.exp(-jnp.abs(z)))


def _shift_rows(z, tail, k):
    rolled = pltpu.roll(z, k, 0)
    head = rolled[:SUBLANES]
    rid = lax.broadcasted_iota(jnp.int32, head.shape, 0)
    head = jnp.where(rid < k, pltpu.roll(tail, k, 0), head)
    if z.shape[0] == SUBLANES:
        return head
    return jnp.concatenate([head, rolled[SUBLANES:]], axis=0)


def _causal_conv(z, tail, w):
    return w[2:3] * z + w[1:2] * _shift_rows(z, tail, 1) + w[0:1] * _shift_rows(z, tail, 2)


def _const_spec(shape):
    nd = len(shape)
    return pl.BlockSpec(shape, lambda b, t: (0,) * nd, pipeline_mode=pl.Buffered(1))


def _init_spec(shape, shared):
    nd = len(shape)
    block = (1,) + tuple(shape[1:])
    if shared:
        return pl.BlockSpec(block, lambda b, t: (0,) * nd)
    return pl.BlockSpec(block, lambda b, t: (b,) + (0,) * (nd - 1))


def _tile_spec(tm, d):
    return pl.BlockSpec((1, tm, d), lambda b, t: (b, t, 0))


def _next_tile_spec(tm, d, n_t):
    return pl.BlockSpec((1, tm, d), lambda b, t: (b, jnp.minimum(t + 1, n_t - 1), 0))


def _compiler_params(flags=None):
    return pltpu.CompilerParams(dimension_semantics=("arbitrary", "arbitrary"),
                                vmem_limit_bytes=VMEM_LIMIT_BYTES, flags=flags)


def _ffn_kernel(x_ref, init_ref, gn_ref, wup_ref, cw_ref, cb_ref, wd_ref,
                gfin_ref, o_ref, cache_ref, *z_bufs, final_norm):
    t = pl.program_id(1)
    n_t = pl.num_programs(1)
    n_c = len(z_bufs) // 2
    zg_bufs, zu_bufs = z_bufs[:n_c], z_bufs[n_c:]
    fc = zg_bufs[0].shape[1]
    f = n_c * fc
    tm = x_ref.shape[1]
    s8 = SUBLANES

    @pl.when(t == 0)
    def _():
        for j in range(n_c):
            zg_bufs[j][0:s8, :] = init_ref[0, 0, j]
            zu_bufs[j][0:s8, :] = init_ref[0, 1, j]

    x = x_ref[0]
    h = _rmsnorm(x, gn_ref[...]).astype(_BF16)

    def up(j):
        zg_bufs[j][s8:s8 + tm, :] = _dot(h, wup_ref[:, j * fc:(j + 1) * fc])
        zu_bufs[j][s8:s8 + tm, :] = _dot(h, wup_ref[:, f + j * fc:f + (j + 1) * fc])

    sub = lax.broadcasted_iota(jnp.int32, (1, s8, fc), 1)

    def conv(buf, c0):
        w = cw_ref[:, c0:c0 + fc]
        cur = buf[s8:s8 + tm, :].reshape(tm // s8, s8, fc)
        head = buf[0:s8, :].reshape(1, s8, fc)
        shifted = []
        for k in (1, 2):
            rot = pltpu.roll(cur, k, 1)
            prev = jnp.concatenate([pltpu.roll(head, k, 1), rot[:-1]], axis=0)
            shifted.append(jnp.where(sub < k, prev, rot))
        y = w[2:3] * cur + w[1:2] * shifted[0] + w[0:1] * shifted[1] + cb_ref[:, c0:c0 + fc]
        return y.reshape(tm, fc)

    def act(j):
        a = _silu(conv(zg_bufs[j], j * fc)) * conv(zu_bufs[j], f + j * fc)
        zg_bufs[j][0:s8, :] = zg_bufs[j][tm:tm + s8, :]
        zu_bufs[j][0:s8, :] = zu_bufs[j][tm:tm + s8, :]
        return a.astype(_BF16)

    for j in range(n_c):
        up(j)
    acts = [act(j) for j in range(n_c)]
    rb = min(tm, FFN_DOWN_ROWS)
    for r0 in range(0, tm, rb):
        acc = x[r0:r0 + rb]
        for j in range(n_c):
            acc = acc + _dot(acts[j][r0:r0 + rb], wd_ref[j * fc:(j + 1) * fc, :])
        if final_norm:
            acc = _rmsnorm(acc, gfin_ref[...])
        o_ref[0, r0:r0 + rb, :] = acc

    @pl.when(t == n_t - 1)
    def _():
        for j in range(n_c):
            cache_ref[0, 0, j] = zg_bufs[j][0:s8, :]
            cache_ref[0, 1, j] = zu_bufs[j][0:s8, :]


def _ffn_call(x, init, w, gfin, *, n_seq, tm, final_norm):
    _, seq, d = x.shape
    fc = FFN_COLS
    n_c = w["wd"].shape[0] // fc
    shared = init.shape[0] == 1 and n_seq > 1
    tail_shape = (n_c, SUBLANES, fc)
    out, cache = pl.pallas_call(
        functools.partial(_ffn_kernel, final_norm=final_norm),
        grid=(n_seq, seq // tm),
        in_specs=[
            _tile_spec(tm, d),
            _init_spec(init.shape, shared),
            _const_spec(w["gn"].shape),
            _const_spec(w["wup"].shape), _const_spec(w["cw"].shape), _const_spec(w["cb"].shape),
            _const_spec(w["wd"].shape),
            _const_spec(gfin.shape),
        ],
        out_specs=[
            _tile_spec(tm, d),
            pl.BlockSpec((1, 2) + tail_shape, lambda b, t: (b, 0, 0, 0, 0)),
        ],
        out_shape=[
            jax.ShapeDtypeStruct((n_seq, seq, d), _F32),
            jax.ShapeDtypeStruct((n_seq, 2) + tail_shape, _F32),
        ],
        scratch_shapes=[pltpu.VMEM((SUBLANES + tm, fc), _F32) for _ in range(2 * n_c)],
        compiler_params=_compiler_params(),
        name="conv_ffn",
    )(x, init, w["gn"], w["wup"], w["cw"], w["cb"], w["wd"], gfin)
    return out, cache


def _sc_kernel(x_ref, init_ref, gn_ref, win_ref, cw_ref, wout_ref, o_ref, cache_ref, tail):
    t = pl.program_id(1)
    n_t = pl.num_programs(1)
    tm, d = x_ref.shape[1], x_ref.shape[2]
    wc = SC_COLS

    @pl.when(t == 0)
    def _():
        tail[...] = init_ref[0]

    x = x_ref[0]
    h = _rmsnorm(x, gn_ref[...]).astype(_BF16)
    blocks = [slice(c0, c0 + wc) for c0 in range(0, d, wc)]
    proj = [[_dot(h, win_ref[:, i * d + c.start:i * d + c.stop]) for i in range(3)] for c in blocks]
    gated = []
    for c, (p_b, p_c, p_u) in zip(blocks, proj):
        ci = p_c * p_u
        y = _causal_conv(ci, tail[:, c], cw_ref[:, c])
        tail[:, c] = ci[tm - SUBLANES:]
        gated.append((p_b * y).astype(_BF16))
    acc = x
    for c, a in zip(blocks, gated):
        acc = acc + _dot(a, wout_ref[c, :])
    o_ref[0] = acc

    @pl.when(t == n_t - 1)
    def _():
        cache_ref[0] = tail[...]


def _sc_call(x, init, w, *, n_seq, tm):
    _, seq, d = x.shape
    shared = init.shape[0] == 1 and n_seq > 1
    out, cache = pl.pallas_call(
        _sc_kernel,
        grid=(n_seq, seq // tm),
        in_specs=[
            _tile_spec(tm, d),
            _init_spec(init.shape, shared),
            _const_spec(w["gn"].shape), _const_spec(w["win"].shape),
            _const_spec(w["cw"].shape), _const_spec(w["wout"].shape),
        ],
        out_specs=[
            _tile_spec(tm, d),
            pl.BlockSpec((1, SUBLANES, d), lambda b, t: (b, 0, 0)),
        ],
        out_shape=[
            jax.ShapeDtypeStruct((n_seq, seq, d), _F32),
            jax.ShapeDtypeStruct((n_seq, SUBLANES, d), _F32),
        ],
        scratch_shapes=[pltpu.VMEM((SUBLANES, d), _F32)],
        compiler_params=_compiler_params(),
        name="shortconv_mixer",
    )(x, init, w["gn"], w["win"], w["cw"], w["wout"])
    return out, cache


def _cumsum_rows(a):
    n = a.shape[0]
    rid = lax.broadcasted_iota(jnp.int32, (SUBLANES, a.shape[1]), 0)
    groups = []
    for i in range(0, n, SUBLANES):
        grp = a[i:i + SUBLANES]
        s = 1
        while s < SUBLANES:
            grp = grp + jnp.where(rid >= s, pltpu.roll(grp, s, 0), 0.0)
            s *= 2
        if groups:
            grp = grp + groups[-1][SUBLANES - 1:SUBLANES]
        groups.append(grp)
    return jnp.concatenate(groups, axis=0) if len(groups) > 1 else groups[0]


def _level_masks(c):
    tt = lax.broadcasted_iota(jnp.int32, (c, c), 0)
    ss = lax.broadcasted_iota(jnp.int32, (c, c), 1)
    txs = tt ^ ss
    lower = tt > ss
    masks = [tt == ss]
    half = 1
    while half < c:
        masks.append(lower & (txs >= half) & (txs < 2 * half))
        half *= 2
    return masks


def _gla_chunk_scores(q, k, g, c, masks, mild):
    n_h = GLA_HEADS
    dk = q.shape[1] // n_h
    b = _cumsum_rows(g)
    b_last = b[c - 1:c]
    qe = (q * jnp.exp(b)).astype(_BF16)
    kd = (k * jnp.exp(b_last - b)).astype(_BF16)
    decay = jnp.exp(b_last)

    if mild:
        k_inv = (k * jnp.exp(-b)).astype(_BF16)
        tt = lax.broadcasted_iota(jnp.int32, (c, c), 0)
        causal = tt >= lax.broadcasted_iota(jnp.int32, (c, c), 1)
        scores = []
        for hh in range(n_h):
            ks = slice(hh * dk, (hh + 1) * dk)
            a = jnp.where(causal, _dot_nt(qe[:, ks], k_inv[:, ks]), 0.0)
            scores.append(a.astype(_BF16))
        return dict(scores=scores, qe=qe, kd=kd, decay=decay)

    rid = lax.broadcasted_iota(jnp.int32, b.shape, 0)
    levels = [(q.astype(_BF16), k.astype(_BF16))]
    e_k = b
    half = 1
    while half < c:
        hi = (rid & half) != 0
        b_mid = jnp.where(hi, pltpu.roll(e_k, half, 0), e_k)
        scale = jnp.exp2((b - b_mid) * jnp.where(hi, _LOG2E, -_LOG2E))
        x = (jnp.where(hi, q, k) * scale).astype(_BF16)
        levels.append((x, x))
        if 2 * half < c:
            e_k = jnp.where(hi, e_k, pltpu.roll(e_k, c - half, 0))
        half *= 2

    scores = []
    for hh in range(n_h):
        ks = slice(hh * dk, (hh + 1) * dk)
        a = jnp.zeros((c, c), _F32)
        for mask, (qs, kk) in zip(masks, levels):
            a = jnp.where(mask, _dot_nt(qs[:, ks], kk[:, ks]), a)
        scores.append(a.astype(_BF16))
    return dict(scores=scores, qe=qe, kd=kd, decay=decay)


def _gla_chunk_apply(pre, v, s_ref, hg):
    n_h = GLA_HEADS
    qe, kd, decay = pre["qe"], pre["kd"], pre["decay"]
    dk = qe.shape[1] // n_h
    dv = v.shape[1] // n_h
    outs = []
    for hh in range(n_h):
        ks = slice(hh * dk, (hh + 1) * dk)
        vs = slice(hh * dv, (hh + 1) * dv)
        s_old = s_ref[hh]
        o = _dot(qe[:, ks], s_old.astype(_BF16)) + _dot(pre["scores"][hh], v[:, vs])
        o = o * lax.rsqrt(jnp.mean(o * o, axis=-1, keepdims=True) + EPS) * hg[:, vs]
        outs.append(o)
        dcol = jnp.transpose(jnp.broadcast_to(decay[:, ks], (dk, dk)))
        dmat = jnp.concatenate([dcol] * (dv // dk), axis=1)
        s_ref[hh] = dmat * s_old + _dot_tn(kd[:, ks], v[:, vs])
    return jnp.concatenate(outs, axis=1)


def _gla_kernel(x_ref, s0_ref, gn_ref, win_ref, wgd_ref, wgu_ref, bgate_ref, hg_ref, wout_ref,
                o_ref, sout_ref, s_scr, p_scr, g_scr, o_scr, h_scr, *, chunk):
    t = pl.program_id(1)
    n_t = pl.num_programs(1)
    tm = x_ref.shape[1]
    hk = g_scr.shape[1]
    hv = o_scr.shape[1]
    n_chunks = tm // chunk

    @pl.when(t == 0)
    def _():
        s_scr[...] = s0_ref[0]

    h_scr[...] = _rmsnorm(x_ref[0], gn_ref[...]).astype(_BF16)
    h = h_scr[...]
    gdown = _dot(h, wgd_ref[...]).astype(_BF16)
    z = _dot(gdown, wgu_ref[...]) + bgate_ref[...]
    g = _log_sigmoid(z) * (1.0 / GLA_TAU)
    g_scr[...] = g
    p_scr[:, 0:2 * hk] = _dot(h, win_ref[:, 0:2 * hk])
    chunk_sums = [jnp.sum(g[ci * chunk:(ci + 1) * chunk], axis=0, keepdims=True)
                  for ci in range(n_chunks)]
    lowest = jnp.min(functools.reduce(jnp.minimum, chunk_sums))
    q_scale = float(hk // GLA_HEADS) ** -0.5

    def step_body(mild):
        masks = _level_masks(chunk)
        hb = h_scr[...]
        half = hv // 2
        pieces = [(2 * hk + i * half, 2 * hk + (i + 1) * half) for i in range(4)]

        def project(i):
            c0, c1 = pieces[i]
            p_scr[:, c0:c1] = _dot(hb, win_ref[:, c0:c1])

        def scores(ci):
            rows = slice(ci * chunk, (ci + 1) * chunk)
            q = p_scr[rows, 0:hk] * q_scale
            k = p_scr[rows, hk:2 * hk]
            return _gla_chunk_scores(q, k, g_scr[rows, :], chunk, masks, mild)

        def apply(ci, pre):
            rows = slice(ci * chunk, (ci + 1) * chunk)
            v = p_scr[rows, 2 * hk:2 * hk + hv].astype(_BF16)
            o_scr[rows, :] = _gla_chunk_apply(pre, v, s_scr, hg_ref[...])

        project(0)
        pre = scores(0)
        project(1)
        for ci in range(n_chunks):
            nxt_pre = scores(ci + 1) if ci + 1 < n_chunks else None
            apply(ci, pre)
            if ci < 2:
                project(2 + ci)
            pre = nxt_pre
        for i in range(2 + min(2, n_chunks), 4):
            project(i)
        r = p_scr[:, 2 * hk + hv:2 * hk + 2 * hv]
        o_ref[0] = x_ref[0] + _dot((o_scr[...] * _silu(r)).astype(_BF16), wout_ref[...])

    is_mild = lowest >= MILD_LOG_DECAY

    @pl.when(is_mild)
    def _():
        step_body(True)

    @pl.when(jnp.logical_not(is_mild))
    def _():
        step_body(False)

    @pl.when(t == n_t - 1)
    def _():
        sout_ref[0] = s_scr[...]


def _gla_call(x, s0, w, *, n_seq, tm, chunk):
    _, seq, d = x.shape
    n_h, dk, dv = s0.shape[1:]
    hk, hv = n_h * dk, n_h * dv
    shared = s0.shape[0] == 1 and n_seq > 1
    out, s_out = pl.pallas_call(
        functools.partial(_gla_kernel, chunk=chunk),
        grid=(n_seq, seq // tm),
        in_specs=[
            _tile_spec(tm, d),
            _init_spec(s0.shape, shared),
            _const_spec(w["gn"].shape), _const_spec(w["win"].shape),
            _const_spec(w["wgd"].shape), _const_spec(w["wgu"].shape),
            _const_spec(w["bgate"].shape), _const_spec(w["hg"].shape),
            _const_spec(w["wout"].shape),
        ],
        out_specs=[
            _tile_spec(tm, d),
            pl.BlockSpec((1, n_h, dk, dv), lambda b, t: (b, 0, 0, 0)),
        ],
        out_shape=[
            jax.ShapeDtypeStruct((n_seq, seq, d), _F32),
            jax.ShapeDtypeStruct((n_seq, n_h, dk, dv), _F32),
        ],
        scratch_shapes=[
            pltpu.VMEM((n_h, dk, dv), _F32),
            pltpu.VMEM((tm, 2 * hk + 2 * hv), _F32),
            pltpu.VMEM((tm, hk), _F32),
            pltpu.VMEM((tm, hv), _F32),
            pltpu.VMEM((tm, d), _BF16),
        ],
        compiler_params=_compiler_params(),
        name="gla_mixer",
    )(x, s0, w["gn"], w["win"], w["wgd"], w["wgu"], w["bgate"], w["hg"], w["wout"])
    return out, s_out


SHORT_STRIDE = 32


def _short_spec(shape):
    nd = len(shape)
    return pl.BlockSpec(shape, lambda i: (0,) * nd, pipeline_mode=pl.Buffered(1))


def _short_params():
    return pltpu.CompilerParams(dimension_semantics=("arbitrary",), vmem_limit_bytes=VMEM_LIMIT_BYTES)


def _token_rows(n_rows, width, n_tok):
    pos = lax.broadcasted_iota(jnp.int32, (n_rows, width), 0) & (SHORT_STRIDE - 1)
    return pos >= SHORT_STRIDE - n_tok


def _roll_conv(z, w):
    return w[2:3] * z + w[1:2] * pltpu.roll(z, 1, 0) + w[0:1] * pltpu.roll(z, 2, 0)


def _last_rows(z, n_seg):
    return [z[(s + 1) * SHORT_STRIDE - SUBLANES:(s + 1) * SHORT_STRIDE] for s in range(n_seg)]


def _ffn_short_kernel(x_ref, inj_ref, gn_ref, wup_ref, cw_ref, cb_ref, wd_ref, gfin_ref,
                      o_ref, tail_ref, *, final_norm, n_tok):
    n_seg = tail_ref.shape[0]
    f = wd_ref.shape[0]
    fc = FFN_COLS
    x = x_ref[...]
    h = _rmsnorm(x, gn_ref[...]).astype(_BF16)
    acc = x
    for j in range(f // fc):
        halves = []
        for c0 in (j * fc, f + j * fc):
            cols = slice(c0, c0 + fc)
            z = _dot(h, wup_ref[:, cols]) + inj_ref[:, cols]
            for s, rows in enumerate(_last_rows(z, n_seg)):
                tail_ref[s, :, cols] = rows
            halves.append(_roll_conv(z, cw_ref[:, cols]) + cb_ref[:, cols])
        a = (_silu(halves[0]) * halves[1]).astype(_BF16)
        acc = acc + _dot(a, wd_ref[j * fc:(j + 1) * fc, :])
    if final_norm:
        acc = _rmsnorm(acc, gfin_ref[...])
    o_ref[...] = jnp.where(_token_rows(x.shape[0], x.shape[1], n_tok), acc, 0.0)


def _ffn_short_call(x, inj, w, gfin, *, n_seg, n_tok, final_norm):
    rows, d = x.shape
    f2 = w["wup"].shape[1]
    args = (x, inj, w["gn"], w["wup"], w["cw"], w["cb"], w["wd"], gfin)
    return pl.pallas_call(
        functools.partial(_ffn_short_kernel, final_norm=final_norm, n_tok=n_tok),
        grid=(1,),
        in_specs=[_short_spec(a.shape) for a in args],
        out_specs=[_short_spec((rows, d)), _short_spec((n_seg, SUBLANES, f2))],
        out_shape=[jax.ShapeDtypeStruct((rows, d), _F32),
                   jax.ShapeDtypeStruct((n_seg, SUBLANES, f2), _F32)],
        compiler_params=_short_params(),
        name="conv_ffn_short",
    )(*args)


def _sc_short_kernel(x_ref, inj_ref, gn_ref, win_ref, cw_ref, wout_ref, o_ref, tail_ref, *, n_tok):
    n_seg = tail_ref.shape[0]
    x = x_ref[...]
    d = x.shape[1]
    h = _rmsnorm(x, gn_ref[...]).astype(_BF16)
    bg = _dot(h, win_ref[:, 0:d])
    ci = _dot(h, win_ref[:, d:2 * d]) * _dot(h, win_ref[:, 2 * d:3 * d]) + inj_ref[...]
    for s, rows in enumerate(_last_rows(ci, n_seg)):
        tail_ref[s] = rows
    y = _roll_conv(ci, cw_ref[...])
    out = x + _dot((bg * y).astype(_BF16), wout_ref[...])
    o_ref[...] = jnp.where(_token_rows(x.shape[0], d, n_tok), out, 0.0)


def _sc_short_call(x, inj, w, *, n_seg, n_tok):
    rows, d = x.shape
    args = (x, inj, w["gn"], w["win"], w["cw"], w["wout"])
    return pl.pallas_call(
        functools.partial(_sc_short_kernel, n_tok=n_tok),
        grid=(1,),
        in_specs=[_short_spec(a.shape) for a in args],
        out_specs=[_short_spec((rows, d)), _short_spec((n_seg, SUBLANES, d))],
        out_shape=[jax.ShapeDtypeStruct((rows, d), _F32),
                   jax.ShapeDtypeStruct((n_seg, SUBLANES, d), _F32)],
        compiler_params=_short_params(),
        name="shortconv_mixer_short",
    )(*args)


def _gla_short_kernel(x_ref, s0_ref, gn_ref, win_ref, wgd_ref, wgu_ref, bgate_ref, hg_ref, wout_ref,
                      o_ref, sout_ref, s_scr, p_scr, g_scr, o_scr, *, n_tok):
    n_seg = s0_ref.shape[0]
    c = SHORT_STRIDE
    hk = g_scr.shape[1]
    hv = o_scr.shape[1]
    x = x_ref[...]
    h = _rmsnorm(x, gn_ref[...]).astype(_BF16)
    p_scr[...] = _dot(h, win_ref[...])
    gdown = _dot(h, wgd_ref[...]).astype(_BF16)
    z = _dot(gdown, wgu_ref[...]) + bgate_ref[...]
    g = jnp.where(_token_rows(x.shape[0], hk, n_tok), _log_sigmoid(z) * (1.0 / GLA_TAU), 0.0)
    g_scr[...] = g
    sums = [jnp.sum(g[s * c:(s + 1) * c], axis=0, keepdims=True) for s in range(n_seg)]
    is_mild = jnp.min(functools.reduce(jnp.minimum, sums)) >= MILD_LOG_DECAY
    q_scale = float(hk // GLA_HEADS) ** -0.5
    masks = _level_masks(c)

    def walk(mild):
        def body(s, carry):
            rows = pl.ds(pl.multiple_of(s * c, c), c)
            q = p_scr[rows, 0:hk] * q_scale
            k = p_scr[rows, hk:2 * hk]
            v = p_scr[rows, 2 * hk:2 * hk + hv].astype(_BF16)
            s_scr[...] = s0_ref[s]
            pre = _gla_chunk_scores(q, k, g_scr[rows, :], c, masks, mild)
            o_scr[rows, :] = _gla_chunk_apply(pre, v, s_scr, hg_ref[...])
            sout_ref[s] = s_scr[...]
            return carry

        lax.fori_loop(0, n_seg, body, 0)

    @pl.when(is_mild)
    def _():
        walk(True)

    @pl.when(jnp.logical_not(is_mild))
    def _():
        walk(False)

    r = p_scr[:, 2 * hk + hv:2 * hk + 2 * hv]
    out = x + _dot((o_scr[...] * _silu(r)).astype(_BF16), wout_ref[...])
    o_ref[...] = jnp.where(_token_rows(x.shape[0], x.shape[1], n_tok), out, 0.0)


def _gla_short_call(x, s0, w, *, n_tok):
    rows, d = x.shape
    n_seg, n_h, dk, dv = s0.shape
    hk, hv = n_h * dk, n_h * dv
    args = (x, s0, w["gn"], w["win"], w["wgd"], w["wgu"], w["bgate"], w["hg"], w["wout"])
    return pl.pallas_call(
        functools.partial(_gla_short_kernel, n_tok=n_tok),
        grid=(1,),
        in_specs=[_short_spec(a.shape) for a in args],
        out_specs=[_short_spec((rows, d)), _short_spec(s0.shape)],
        out_shape=[jax.ShapeDtypeStruct((rows, d), _F32), jax.ShapeDtypeStruct(s0.shape, _F32)],
        scratch_shapes=[
            pltpu.VMEM((n_h, dk, dv), _F32),
            pltpu.VMEM((rows, 2 * hk + 2 * hv), _F32),
            pltpu.VMEM((rows, hk), _F32),
            pltpu.VMEM((rows, hv), _F32),
        ],
        compiler_params=_short_params(),
        name="gla_mixer_short",
    )(*args)


def _short_rows(a):
    n_seg, n_tok, width = a.shape
    return jnp.pad(a, ((0, 0), (SHORT_STRIDE - n_tok, 0), (0, 0))).reshape(n_seg * SHORT_STRIDE, width)


def _short_inject(cache, n_tok):
    n_seg, n_prev, width = cache.shape
    lead = SHORT_STRIDE - n_tok - n_prev
    return jnp.pad(cache, ((0, 0), (lead, n_tok), (0, 0))).reshape(n_seg * SHORT_STRIDE, width)


def _short_trunk(xs, gla_s, conv_c, ffn_c, params, gfin):
    n_seg, n_tok, d = xs.shape
    depth = len(params)
    x = _short_rows(xs)
    keep = lambda tail: tail[:, SUBLANES - (CONV_W - 1):, :]
    gla_new, conv_new, ffn_new = [], [], []
    for i, (mix, ffn) in enumerate(params):
        if i % 2 == 0:
            x, s = _gla_short_call(x, gla_s[i // 2], mix, n_tok=n_tok)
            gla_new.append(s)
        else:
            x, tail = _sc_short_call(x, _short_inject(conv_c[i // 2], n_tok), mix,
                                     n_seg=n_seg, n_tok=n_tok)
            conv_new.append(keep(tail))
        x, tail = _ffn_short_call(x, _short_inject(ffn_c[i], n_tok), ffn, gfin, n_seg=n_seg,
                                  n_tok=n_tok, final_norm=(i == depth - 1))
        ffn_new.append(keep(tail))
    y = x.reshape(n_seg, SHORT_STRIDE, d)[:, SHORT_STRIDE - n_tok:, :]
    return y, gla_new, conv_new, ffn_new


def _row(v):
    return v.reshape(1, -1).astype(_F32)


def _prep_gla(norm, w_in, w_gate_up, b_gate, head_gain, w_out):
    main = w_in.shape[1] - GLA_GATE_RANK
    wgd = jnp.pad(w_in[:, main:], ((0, 0), (0, LANES - GLA_GATE_RANK)))
    wgu = jnp.pad(w_gate_up, ((0, LANES - GLA_GATE_RANK), (0, 0)))
    return dict(gn=_row(norm), win=w_in[:, :main].astype(_BF16), wgd=wgd.astype(_BF16),
                wgu=wgu.astype(_BF16), bgate=_row(b_gate), hg=_row(head_gain),
                wout=w_out.astype(_BF16))


def _prep_sc(norm, w_in, conv_w, w_out):
    return dict(gn=_row(norm), win=w_in.astype(_BF16), cw=conv_w.astype(_F32),
                wout=w_out.astype(_BF16))


def _prep_ffn(norm, w_up, conv_w, conv_b, w_down):
    assert w_down.shape[0] % FFN_COLS == 0
    return dict(gn=_row(norm), wup=w_up.astype(_BF16), cw=conv_w.astype(_F32), cb=_row(conv_b),
                wd=w_down.astype(_BF16))


def _tail_from_cache(cache):
    return jnp.pad(cache, ((0, 0), (SUBLANES - (CONV_W - 1), 0), (0, 0)))


def _ffn_tail_from_cache(cache, n_c, fc):
    b = cache.shape[0]
    t = _tail_from_cache(cache).reshape(b, SUBLANES, 2, n_c, fc)
    return t.transpose(0, 2, 3, 1, 4)


def _ffn_cache_from_tail(tail):
    b = tail.shape[0]
    rows = tail[:, :, :, SUBLANES - (CONV_W - 1):, :]
    return rows.transpose(0, 3, 1, 2, 4).reshape(b, CONV_W - 1, -1)


def _trunk(x, gla_s, conv_tail, ffn_tail, params, gfin, *, tm_gla, tm_sc, tm_ffn, chunk):
    n_seq = x.shape[0]
    depth = len(params)
    gla_new, conv_new, ffn_new = [], [], []
    i_gla = i_conv = 0
    for i, (mix, ffn) in enumerate(params):
        if i % 2 == 0:
            x, s = _gla_call(x, gla_s[i_gla], mix, n_seq=n_seq, tm=tm_gla, chunk=chunk)
            gla_new.append(s)
            i_gla += 1
        else:
            x, s = _sc_call(x, conv_tail[i_conv], mix, n_seq=n_seq, tm=tm_sc)
            conv_new.append(s)
            i_conv += 1
        x, s = _ffn_call(x, ffn_tail[i], ffn, gfin, n_seq=n_seq, tm=tm_ffn,
                         final_norm=(i == depth - 1))
        ffn_new.append(s)
    return x, gla_new, conv_new, ffn_new


def kernel(x_prompt, x_sample, state_gla, cache_conv, cache_ffn, meta, norm_mix, norm_ffn, norm_final,
           gla_w_in, gla_w_gate_up, gla_b_gate, gla_head_gain, gla_w_out,
           sc_w_in, sc_conv_w, sc_w_out, ffn_w_up, ffn_conv_w, ffn_conv_b, ffn_w_down):
    depth = norm_mix.shape[0]
    n_b, seq, d = x_prompt.shape
    n_s, s_len, _ = x_sample.shape
    assert s_len == N_META and seq % GLA_CHUNK == 0
    f = ffn_w_down.shape[1]
    n_c, fc = f // FFN_COLS, FFN_COLS

    params = []
    for i in range(depth):
        j = i // 2
        if i % 2 == 0:
            mix = _prep_gla(norm_mix[i], gla_w_in[j], gla_w_gate_up[j], gla_b_gate[j],
                            gla_head_gain[j], gla_w_out[j])
        else:
            mix = _prep_sc(norm_mix[i], sc_w_in[j], sc_conv_w[j], sc_w_out[j])
        params.append((mix, _prep_ffn(norm_ffn[i], ffn_w_up[i], ffn_conv_w[i], ffn_conv_b[i],
                                      ffn_w_down[i])))
    gfin = _row(norm_final)

    xs = jnp.concatenate([x_sample, meta[None].astype(x_sample.dtype)], axis=0)
    zero1 = lambda a: jnp.zeros((a.shape[0], 1) + a.shape[2:], a.dtype)
    gla_s = jnp.concatenate([state_gla, zero1(state_gla)], axis=1).astype(_F32)
    conv_c = jnp.concatenate([cache_conv, zero1(cache_conv)], axis=1).astype(_F32)
    ffn_c = jnp.concatenate([cache_ffn, zero1(cache_ffn)], axis=1).astype(_F32)
    ys, gla_1, conv_1, ffn_1 = _short_trunk(xs, gla_s, conv_c, ffn_c, params, gfin)

    yp, gla_2, conv_2, ffn_2 = _trunk(
        x_prompt, [s[n_s:] for s in gla_1],
        [_tail_from_cache(s[n_s:]) for s in conv_1],
        [_ffn_tail_from_cache(s[n_s:], n_c, fc) for s in ffn_1],
        params, gfin, tm_gla=min(TILE_GLA, seq), tm_sc=min(TILE_SC, seq), tm_ffn=min(TILE_FFN, seq),
        chunk=GLA_CHUNK)

    dt = x_prompt.dtype
    tail2 = lambda s: s[:, SUBLANES - (CONV_W - 1):, :]
    return (yp.astype(dt), ys[:n_s].astype(dt),
            jnp.stack(gla_2).astype(dt), jnp.stack([s[:n_s] for s in gla_1]).astype(dt),
            jnp.stack([tail2(s) for s in conv_2]).astype(dt),
            jnp.stack([s[:n_s] for s in conv_1]).astype(dt),
            jnp.stack([_ffn_cache_from_tail(s) for s in ffn_2]).astype(dt),
            jnp.stack([s[:n_s] for s in ffn_1]).astype(dt))
```

```python
import functools

import jax
import jax.numpy as jnp
from jax import lax
from jax.experimental import pallas as pl
from jax.experimental.pallas import tpu as pltpu

N_PREFIX = 16
GLA_HEADS = 4
GLA_GATE_RANK = 16
GLA_TAU = 16.0
CONV_W = 3
EPS = 1e-6
GLA_CHUNK = 256
MILD_LOG_DECAY = -60.0
FFN_COLS = 256
SC_COLS = 256
FFN_DOWN_ROWS = 256
TILE_GLA = 512
TILE_SC = 512
TILE_FFN = 512
LANES = 128
SUBLANES = 8
VMEM_LIMIT_BYTES = 56 * 1024 * 1024

_BF16 = jnp.bfloat16
_F32 = jnp.float32
_LOG2E = 1.4426950408889634


def _dot(a, b):
    return lax.dot_general(a, b, (((1,), (0,)), ((), ())), preferred_element_type=_F32)


def _dot_nt(a, b):
    return lax.dot_general(a, b, (((1,), (1,)), ((), ())), preferred_element_type=_F32)


def _dot_tn(a, b):
    return lax.dot_general(a, b, (((0,), (0,)), ((), ())), preferred_element_type=_F32)


def _rmsnorm(x, g):
    return x * lax.rsqrt(jnp.mean(x * x, axis=-1, keepdims=True) + EPS) * g


def _silu(x):
    return x / (1.0 + jnp.exp2(x * (-_LOG2E)))


def _log_sigmoid(z):
    return jnp.minimum(z, 0.0) - jnp.log(1.0 + jnp.exp(-jnp.abs(z)))


def _shift_rows(z, tail, k):
    rolled = pltpu.roll(z, k, 0)
    head = rolled[:SUBLANES]
    rid = lax.broadcasted_iota(jnp.int32, head.shape, 0)
    head = jnp.where(rid < k, pltpu.roll(tail, k, 0), head)
    if z.shape[0] == SUBLANES:
        return head
    return jnp.concatenate([head, rolled[SUBLANES:]], axis=0)


def _causal_conv(z, tail, w):
    return w[2:3] * z + w[1:2] * _shift_rows(z, tail, 1) + w[0:1] * _shift_rows(z, tail, 2)


def _const_spec(shape):
    nd = len(shape)
    return pl.BlockSpec(shape, lambda b, t: (0,) * nd, pipeline_mode=pl.Buffered(1))


def _init_spec(shape, shared):
    nd = len(shape)
    block = (1,) + tuple(shape[1:])
    if shared:
        return pl.BlockSpec(block, lambda b, t: (0,) * nd)
    return pl.BlockSpec(block, lambda b, t: (b,) + (0,) * (nd - 1))


def _tile_spec(tm, d):
    return pl.BlockSpec((1, tm, d), lambda b, t: (b, t, 0))


def _compiler_params():
    return pltpu.CompilerParams(dimension_semantics=("arbitrary", "arbitrary"),
                                vmem_limit_bytes=VMEM_LIMIT_BYTES)


def _ffn_kernel(x_ref, init_ref, gn_ref, wup_ref, cw_ref, cb_ref, wd_ref,
                gfin_ref, o_ref, cache_ref, *z_bufs, final_norm):
    t = pl.program_id(1)
    n_t = pl.num_programs(1)
    n_c = len(z_bufs) // 2
    zg_bufs, zu_bufs = z_bufs[:n_c], z_bufs[n_c:]
    fc = zg_bufs[0].shape[1]
    f = n_c * fc
    tm = x_ref.shape[1]
    s8 = SUBLANES

    @pl.when(t == 0)
    def _():
        for j in range(n_c):
            zg_bufs[j][0:s8, :] = init_ref[0, 0, j]
            zu_bufs[j][0:s8, :] = init_ref[0, 1, j]

    x = x_ref[0]
    h = _rmsnorm(x, gn_ref[...]).astype(_BF16)

    def up(j):
        zg_bufs[j][s8:s8 + tm, :] = _dot(h, wup_ref[:, j * fc:(j + 1) * fc])
        zu_bufs[j][s8:s8 + tm, :] = _dot(h, wup_ref[:, f + j * fc:f + (j + 1) * fc])

    sub = lax.broadcasted_iota(jnp.int32, (1, s8, fc), 1)

    def conv(buf, c0):
        w = cw_ref[:, c0:c0 + fc]
        cur = buf[s8:s8 + tm, :].reshape(tm // s8, s8, fc)
        head = buf[0:s8, :].reshape(1, s8, fc)
        shifted = []
        for k in (1, 2):
            rot = pltpu.roll(cur, k, 1)
            prev = jnp.concatenate([pltpu.roll(head, k, 1), rot[:-1]], axis=0)
            shifted.append(jnp.where(sub < k, prev, rot))
        y = w[2:3] * cur + w[1:2] * shifted[0] + w[0:1] * shifted[1] + cb_ref[:, c0:c0 + fc]
        return y.reshape(tm, fc)

    def act(j):
        a = _silu(conv(zg_bufs[j], j * fc)) * conv(zu_bufs[j], f + j * fc)
        zg_bufs[j][0:s8, :] = zg_bufs[j][tm:tm + s8, :]
        zu_bufs[j][0:s8, :] = zu_bufs[j][tm:tm + s8, :]
        return a.astype(_BF16)

    for j in range(n_c):
        up(j)
    acts = [act(j) for j in range(n_c)]
    rb = min(tm, FFN_DOWN_ROWS)
    for r0 in range(0, tm, rb):
        acc = x[r0:r0 + rb]
        for j in range(n_c):
            acc = acc + _dot(acts[j][r0:r0 + rb], wd_ref[j * fc:(j + 1) * fc, :])
        if final_norm:
            acc = _rmsnorm(acc, gfin_ref[...])
        o_ref[0, r0:r0 + rb, :] = acc

    @pl.when(t == n_t - 1)
    def _():
        for j in range(n_c):
            cache_ref[0, 0, j] = zg_bufs[j][0:s8, :]
            cache_ref[0, 1, j] = zu_bufs[j][0:s8, :]


def _ffn_call(x, init, w, gfin, *, n_seq, tm, final_norm):
    _, seq, d = x.shape
    fc = FFN_COLS
    n_c = w["wd"].shape[0] // fc
    shared = init.shape[0] == 1 and n_seq > 1
    tail_shape = (n_c, SUBLANES, fc)
    out, cache = pl.pallas_call(
        functools.partial(_ffn_kernel, final_norm=final_norm),
        grid=(n_seq, seq // tm),
        in_specs=[
            _tile_spec(tm, d),
            _init_spec(init.shape, shared),
            _const_spec(w["gn"].shape),
            _const_spec(w["wup"].shape), _const_spec(w["cw"].shape), _const_spec(w["cb"].shape),
            _const_spec(w["wd"].shape),
            _const_spec(gfin.shape),
        ],
        out_specs=[
            _tile_spec(tm, d),
            pl.BlockSpec((1, 2) + tail_shape, lambda b, t: (b, 0, 0, 0, 0)),
        ],
        out_shape=[
            jax.ShapeDtypeStruct((n_seq, seq, d), _F32),
            jax.ShapeDtypeStruct((n_seq, 2) + tail_shape, _F32),
        ],
        scratch_shapes=[pltpu.VMEM((SUBLANES + tm, fc), _F32) for _ in range(2 * n_c)],
        compiler_params=_compiler_params(),
        name="conv_ffn",
    )(x, init, w["gn"], w["wup"], w["cw"], w["cb"], w["wd"], gfin)
    return out, cache


def _sc_kernel(x_ref, init_ref, gn_ref, win_ref, cw_ref, wout_ref, o_ref, cache_ref, tail):
    t = pl.program_id(1)
    n_t = pl.num_programs(1)
    tm, d = x_ref.shape[1], x_ref.shape[2]
    wc = SC_COLS

    @pl.when(t == 0)
    def _():
        tail[...] = init_ref[0]

    x = x_ref[0]
    h = _rmsnorm(x, gn_ref[...]).astype(_BF16)
    blocks = [slice(c0, c0 + wc) for c0 in range(0, d, wc)]
    proj = [[_dot(h, win_ref[:, i * d + c.start:i * d + c.stop]) for i in range(3)] for c in blocks]
    gated = []
    for c, (p_b, p_c, p_u) in zip(blocks, proj):
        ci = p_c * p_u
        y = _causal_conv(ci, tail[:, c], cw_ref[:, c])
        tail[:, c] = ci[tm - SUBLANES:]
        gated.append((p_b * y).astype(_BF16))
    acc = x
    for c, a in zip(blocks, gated):
        acc = acc + _dot(a, wout_ref[c, :])
    o_ref[0] = acc

    @pl.when(t == n_t - 1)
    def _():
        cache_ref[0] = tail[...]


def _sc_call(x, init, w, *, n_seq, tm):
    _, seq, d = x.shape
    shared = init.shape[0] == 1 and n_seq > 1
    out, cache = pl.pallas_call(
        _sc_kernel,
        grid=(n_seq, seq // tm),
        in_specs=[
            _tile_spec(tm, d),
            _init_spec(init.shape, shared),
            _const_spec(w["gn"].shape), _const_spec(w["win"].shape),
            _const_spec(w["cw"].shape), _const_spec(w["wout"].shape),
        ],
        out_specs=[
            _tile_spec(tm, d),
            pl.BlockSpec((1, SUBLANES, d), lambda b, t: (b, 0, 0)),
        ],
        out_shape=[
            jax.ShapeDtypeStruct((n_seq, seq, d), _F32),
            jax.ShapeDtypeStruct((n_seq, SUBLANES, d), _F32),
        ],
        scratch_shapes=[pltpu.VMEM((SUBLANES, d), _F32)],
        compiler_params=_compiler_params(),
        name="shortconv_mixer",
    )(x, init, w["gn"], w["win"], w["cw"], w["wout"])
    return out, cache


def _cumsum_rows(a):
    n = a.shape[0]
    rid = lax.broadcasted_iota(jnp.int32, (SUBLANES, a.shape[1]), 0)
    groups = []
    for i in range(0, n, SUBLANES):
        grp = a[i:i + SUBLANES]
        s = 1
        while s < SUBLANES:
            grp = grp + jnp.where(rid >= s, pltpu.roll(grp, s, 0), 0.0)
            s *= 2
        if groups:
            grp = grp + groups[-1][SUBLANES - 1:SUBLANES]
        groups.append(grp)
    return jnp.concatenate(groups, axis=0) if len(groups) > 1 else groups[0]


def _level_masks(c):
    tt = lax.broadcasted_iota(jnp.int32, (c, c), 0)
    ss = lax.broadcasted_iota(jnp.int32, (c, c), 1)
    txs = tt ^ ss
    lower = tt > ss
    masks = [tt == ss]
    half = 1
    while half < c:
        masks.append(lower & (txs >= half) & (txs < 2 * half))
        half *= 2
    return masks


def _gla_chunk_scores(q, k, g, c, masks, mild):
    n_h = GLA_HEADS
    dk = q.shape[1] // n_h
    b = _cumsum_rows(g)
    b_last = b[c - 1:c]
    qe = (q * jnp.exp(b)).astype(_BF16)
    kd = (k * jnp.exp(b_last - b)).astype(_BF16)
    decay = jnp.exp(b_last)

    if mild:
        k_inv = (k * jnp.exp(-b)).astype(_BF16)
        tt = lax.broadcasted_iota(jnp.int32, (c, c), 0)
        causal = tt >= lax.broadcasted_iota(jnp.int32, (c, c), 1)
        scores = []
        for hh in range(n_h):
            ks = slice(hh * dk, (hh + 1) * dk)
            a = jnp.where(causal, _dot_nt(qe[:, ks], k_inv[:, ks]), 0.0)
            scores.append(a.astype(_BF16))
        return dict(scores=scores, qe=qe, kd=kd, decay=decay)

    rid = lax.broadcasted_iota(jnp.int32, b.shape, 0)
    levels = [(q.astype(_BF16), k.astype(_BF16))]
    e_k = b
    half = 1
    while half < c:
        hi = (rid & half) != 0
        b_mid = jnp.where(hi, pltpu.roll(e_k, half, 0), e_k)
        scale = jnp.exp2((b - b_mid) * jnp.where(hi, _LOG2E, -_LOG2E))
        x = (jnp.where(hi, q, k) * scale).astype(_BF16)
        levels.append((x, x))
        if 2 * half < c:
            e_k = jnp.where(hi, e_k, pltpu.roll(e_k, c - half, 0))
        half *= 2

    scores = []
    for hh in range(n_h):
        ks = slice(hh * dk, (hh + 1) * dk)
        a = jnp.zeros((c, c), _F32)
        for mask, (qs, kk) in zip(masks, levels):
            a = jnp.where(mask, _dot_nt(qs[:, ks], kk[:, ks]), a)
        scores.append(a.astype(_BF16))
    return dict(scores=scores, qe=qe, kd=kd, decay=decay)


def _gla_chunk_apply(pre, v, s_ref, hg):
    n_h = GLA_HEADS
    qe, kd, decay = pre["qe"], pre["kd"], pre["decay"]
    dk = qe.shape[1] // n_h
    dv = v.shape[1] // n_h
    outs = []
    for hh in range(n_h):
        ks = slice(hh * dk, (hh + 1) * dk)
        vs = slice(hh * dv, (hh + 1) * dv)
        s_old = s_ref[hh]
        o = _dot(qe[:, ks], s_old.astype(_BF16)) + _dot(pre["scores"][hh], v[:, vs])
        o = o * lax.rsqrt(jnp.mean(o * o, axis=-1, keepdims=True) + EPS) * hg[:, vs]
        outs.append(o)
        dcol = jnp.transpose(jnp.broadcast_to(decay[:, ks], (dk, dk)))
        dmat = jnp.concatenate([dcol] * (dv // dk), axis=1)
        s_ref[hh] = dmat * s_old + _dot_tn(kd[:, ks], v[:, vs])
    return jnp.concatenate(outs, axis=1)


def _gla_kernel(x_ref, s0_ref, gn_ref, win_ref, wgd_ref, wgu_ref, bgate_ref, hg_ref, wout_ref,
                o_ref, sout_ref, s_scr, p_scr, g_scr, o_scr, h_scr, *, chunk):
    t = pl.program_id(1)
    n_t = pl.num_programs(1)
    tm = x_ref.shape[1]
    hk = g_scr.shape[1]
    hv = o_scr.shape[1]
    n_chunks = tm // chunk

    @pl.when(t == 0)
    def _():
        s_scr[...] = s0_ref[0]

    h_scr[...] = _rmsnorm(x_ref[0], gn_ref[...]).astype(_BF16)
    h = h_scr[...]
    gdown = _dot(h, wgd_ref[...]).astype(_BF16)
    z = _dot(gdown, wgu_ref[...]) + bgate_ref[...]
    g = _log_sigmoid(z) * (1.0 / GLA_TAU)
    g_scr[...] = g
    p_scr[:, 0:2 * hk] = _dot(h, win_ref[:, 0:2 * hk])
    chunk_sums = [jnp.sum(g[ci * chunk:(ci + 1) * chunk], axis=0, keepdims=True)
                  for ci in range(n_chunks)]
    lowest = jnp.min(functools.reduce(jnp.minimum, chunk_sums))
    q_scale = float(hk // GLA_HEADS) ** -0.5

    def step_body(mild):
        masks = _level_masks(chunk)
        hb = h_scr[...]
        half = hv // 2
        pieces = [(2 * hk + i * half, 2 * hk + (i + 1) * half) for i in range(4)]

        def project(i):
            c0, c1 = pieces[i]
            p_scr[:, c0:c1] = _dot(hb, win_ref[:, c0:c1])

        def scores(ci):
            rows = slice(ci * chunk, (ci + 1) * chunk)
            q = p_scr[rows, 0:hk] * q_scale
            k = p_scr[rows, hk:2 * hk]
            return _gla_chunk_scores(q, k, g_scr[rows, :], chunk, masks, mild)

        def apply(ci, pre):
            rows = slice(ci * chunk, (ci + 1) * chunk)
            v = p_scr[rows, 2 * hk:2 * hk + hv].astype(_BF16)
            o_scr[rows, :] = _gla_chunk_apply(pre, v, s_scr, hg_ref[...])

        project(0)
        pre = scores(0)
        project(1)
        for ci in range(n_chunks):
            nxt_pre = scores(ci + 1) if ci + 1 < n_chunks else None
            apply(ci, pre)
            if ci < 2:
                project(2 + ci)
            pre = nxt_pre
        for i in range(2 + min(2, n_chunks), 4):
            project(i)
        r = p_scr[:, 2 * hk + hv:2 * hk + 2 * hv]
        o_ref[0] = x_ref[0] + _dot((o_scr[...] * _silu(r)).astype(_BF16), wout_ref[...])

    is_mild = lowest >= MILD_LOG_DECAY

    @pl.when(is_mild)
    def _():
        step_body(True)

    @pl.when(jnp.logical_not(is_mild))
    def _():
        step_body(False)

    @pl.when(t == n_t - 1)
    def _():
        sout_ref[0] = s_scr[...]


def _gla_call(x, s0, w, *, n_seq, tm, chunk):
    _, seq, d = x.shape
    n_h, dk, dv = s0.shape[1:]
    hk, hv = n_h * dk, n_h * dv
    shared = s0.shape[0] == 1 and n_seq > 1
    out, s_out = pl.pallas_call(
        functools.partial(_gla_kernel, chunk=chunk),
        grid=(n_seq, seq // tm),
        in_specs=[
            _tile_spec(tm, d),
            _init_spec(s0.shape, shared),
            _const_spec(w["gn"].shape), _const_spec(w["win"].shape),
            _const_spec(w["wgd"].shape), _const_spec(w["wgu"].shape),
            _const_spec(w["bgate"].shape), _const_spec(w["hg"].shape),
            _const_spec(w["wout"].shape),
        ],
        out_specs=[
            _tile_spec(tm, d),
            pl.BlockSpec((1, n_h, dk, dv), lambda b, t: (b, 0, 0, 0)),
        ],
        out_shape=[
            jax.ShapeDtypeStruct((n_seq, seq, d), _F32),
            jax.ShapeDtypeStruct((n_seq, n_h, dk, dv), _F32),
        ],
        scratch_shapes=[
            pltpu.VMEM((n_h, dk, dv), _F32),
            pltpu.VMEM((tm, 2 * hk + 2 * hv), _F32),
            pltpu.VMEM((tm, hk), _F32),
            pltpu.VMEM((tm, hv), _F32),
            pltpu.VMEM((tm, d), _BF16),
        ],
        compiler_params=_compiler_params(),
        name="gla_mixer",
    )(x, s0, w["gn"], w["win"], w["wgd"], w["wgu"], w["bgate"], w["hg"], w["wout"])
    return out, s_out


SHORT_STRIDE = 32


def _short_spec(shape):
    nd = len(shape)
    return pl.BlockSpec(shape, lambda i: (0,) * nd, pipeline_mode=pl.Buffered(1))


def _short_params():
    return pltpu.CompilerParams(dimension_semantics=("arbitrary",), vmem_limit_bytes=VMEM_LIMIT_BYTES)


def _token_rows(n_rows, width, n_tok):
    pos = lax.broadcasted_iota(jnp.int32, (n_rows, width), 0) & (SHORT_STRIDE - 1)
    return pos >= SHORT_STRIDE - n_tok


def _roll_conv(z, w):
    return w[2:3] * z + w[1:2] * pltpu.roll(z, 1, 0) + w[0:1] * pltpu.roll(z, 2, 0)


def _last_rows(z, n_seg):
    return [z[(s + 1) * SHORT_STRIDE - SUBLANES:(s + 1) * SHORT_STRIDE] for s in range(n_seg)]


def _ffn_short_kernel(x_ref, inj_ref, gn_ref, wup_ref, cw_ref, cb_ref, wd_ref, gfin_ref,
                      o_ref, tail_ref, *, final_norm, n_tok):
    n_seg = tail_ref.shape[0]
    f = wd_ref.shape[0]
    fc = FFN_COLS
    x = x_ref[...]
    h = _rmsnorm(x, gn_ref[...]).astype(_BF16)
    acc = x
    for j in range(f // fc):
        halves = []
        for c0 in (j * fc, f + j * fc):
            cols = slice(c0, c0 + fc)
            z = _dot(h, wup_ref[:, cols]) + inj_ref[:, cols]
            for s, rows in enumerate(_last_rows(z, n_seg)):
                tail_ref[s, :, cols] = rows
            halves.append(_roll_conv(z, cw_ref[:, cols]) + cb_ref[:, cols])
        a = (_silu(halves[0]) * halves[1]).astype(_BF16)
        acc = acc + _dot(a, wd_ref[j * fc:(j + 1) * fc, :])
    if final_norm:
        acc = _rmsnorm(acc, gfin_ref[...])
    o_ref[...] = jnp.where(_token_rows(x.shape[0], x.shape[1], n_tok), acc, 0.0)


def _ffn_short_call(x, inj, w, gfin, *, n_seg, n_tok, final_norm):
    rows, d = x.shape
    f2 = w["wup"].shape[1]
    args = (x, inj, w["gn"], w["wup"], w["cw"], w["cb"], w["wd"], gfin)
    return pl.pallas_call(
        functools.partial(_ffn_short_kernel, final_norm=final_norm, n_tok=n_tok),
        grid=(1,),
        in_specs=[_short_spec(a.shape) for a in args],
        out_specs=[_short_spec((rows, d)), _short_spec((n_seg, SUBLANES, f2))],
        out_shape=[jax.ShapeDtypeStruct((rows, d), _F32),
                   jax.ShapeDtypeStruct((n_seg, SUBLANES, f2), _F32)],
        compiler_params=_short_params(),
        name="conv_ffn_short",
    )(*args)


def _sc_short_kernel(x_ref, inj_ref, gn_ref, win_ref, cw_ref, wout_ref, o_ref, tail_ref, *, n_tok):
    n_seg = tail_ref.shape[0]
    x = x_ref[...]
    d = x.shape[1]
    h = _rmsnorm(x, gn_ref[...]).astype(_BF16)
    bg = _dot(h, win_ref[:, 0:d])
    ci = _dot(h, win_ref[:, d:2 * d]) * _dot(h, win_ref[:, 2 * d:3 * d]) + inj_ref[...]
    for s, rows in enumerate(_last_rows(ci, n_seg)):
        tail_ref[s] = rows
    y = _roll_conv(ci, cw_ref[...])
    out = x + _dot((bg * y).astype(_BF16), wout_ref[...])
    o_ref[...] = jnp.where(_token_rows(x.shape[0], d, n_tok), out, 0.0)


def _sc_short_call(x, inj, w, *, n_seg, n_tok):
    rows, d = x.shape
    args = (x, inj, w["gn"], w["win"], w["cw"], w["wout"])
    return pl.pallas_call(
        functools.partial(_sc_short_kernel, n_tok=n_tok),
        grid=(1,),
        in_specs=[_short_spec(a.shape) for a in args],
        out_specs=[_short_spec((rows, d)), _short_spec((n_seg, SUBLANES, d))],
        out_shape=[jax.ShapeDtypeStruct((rows, d), _F32),
                   jax.ShapeDtypeStruct((n_seg, SUBLANES, d), _F32)],
        compiler_params=_short_params(),
        name="shortconv_mixer_short",
    )(*args)


def _gla_short_kernel(x_ref, s0_ref, gn_ref, win_ref, wgd_ref, wgu_ref, bgate_ref, hg_ref, wout_ref,
                      o_ref, sout_ref, s_scr, p_scr, g_scr, o_scr, *, n_tok):
    n_seg = s0_ref.shape[0]
    c = SHORT_STRIDE
    hk = g_scr.shape[1]
    hv = o_scr.shape[1]
    x = x_ref[...]
    h = _rmsnorm(x, gn_ref[...]).astype(_BF16)
    p_scr[...] = _dot(h, win_ref[...])
    gdown = _dot(h, wgd_ref[...]).astype(_BF16)
    z = _dot(gdown, wgu_ref[...]) + bgate_ref[...]
    g = jnp.where(_token_rows(x.shape[0], hk, n_tok), _log_sigmoid(z) * (1.0 / GLA_TAU), 0.0)
    g_scr[...] = g
    sums = [jnp.sum(g[s * c:(s + 1) * c], axis=0, keepdims=True) for s in range(n_seg)]
    is_mild = jnp.min(functools.reduce(jnp.minimum, sums)) >= MILD_LOG_DECAY
    q_scale = float(hk // GLA_HEADS) ** -0.5
    masks = _level_masks(c)

    def walk(mild):
        def body(s, carry):
            rows = pl.ds(pl.multiple_of(s * c, c), c)
            q = p_scr[rows, 0:hk] * q_scale
            k = p_scr[rows, hk:2 * hk]
            v = p_scr[rows, 2 * hk:2 * hk + hv].astype(_BF16)
            s_scr[...] = s0_ref[s]
            pre = _gla_chunk_scores(q, k, g_scr[rows, :], c, masks, mild)
            o_scr[rows, :] = _gla_chunk_apply(pre, v, s_scr, hg_ref[...])
            sout_ref[s] = s_scr[...]
            return carry

        lax.fori_loop(0, n_seg, body, 0)

    @pl.when(is_mild)
    def _():
        walk(True)

    @pl.when(jnp.logical_not(is_mild))
    def _():
        walk(False)

    r = p_scr[:, 2 * hk + hv:2 * hk + 2 * hv]
    out = x + _dot((o_scr[...] * _silu(r)).astype(_BF16), wout_ref[...])
    o_ref[...] = jnp.where(_token_rows(x.shape[0], x.shape[1], n_tok), out, 0.0)


def _gla_short_call(x, s0, w, *, n_tok):
    rows, d = x.shape
    n_seg, n_h, dk, dv = s0.shape
    hk, hv = n_h * dk, n_h * dv
    args = (x, s0, w["gn"], w["win"], w["wgd"], w["wgu"], w["bgate"], w["hg"], w["wout"])
    return pl.pallas_call(
        functools.partial(_gla_short_kernel, n_tok=n_tok),
        grid=(1,),
        in_specs=[_short_spec(a.shape) for a in args],
        out_specs=[_short_spec((rows, d)), _short_spec(s0.shape)],
        out_shape=[jax.ShapeDtypeStruct((rows, d), _F32), jax.ShapeDtypeStruct(s0.shape, _F32)],
        scratch_shapes=[
            pltpu.VMEM((n_h, dk, dv), _F32),
            pltpu.VMEM((rows, 2 * hk + 2 * hv), _F32),
            pltpu.VMEM((rows, hk), _F32),
            pltpu.VMEM((rows, hv), _F32),
        ],
        compiler_params=_short_params(),
        name="gla_mixer_short",
    )(*args)


def _short_rows(a):
    n_seg, n_tok, width = a.shape
    return jnp.pad(a, ((0, 0), (SHORT_STRIDE - n_tok, 0), (0, 0))).reshape(n_seg * SHORT_STRIDE, width)


def _short_inject(cache, n_tok):
    n_seg, n_prev, width = cache.shape
    lead = SHORT_STRIDE - n_tok - n_prev
    return jnp.pad(cache, ((0, 0), (lead, n_tok), (0, 0))).reshape(n_seg * SHORT_STRIDE, width)


def _short_trunk(xs, gla_s, conv_c, ffn_c, params, gfin):
    n_seg, n_tok, d = xs.shape
    depth = len(params)
    x = _short_rows(xs)
    keep = lambda tail: tail[:, SUBLANES - (CONV_W - 1):, :]
    gla_new, conv_new, ffn_new = [], [], []
    for i, (mix, ffn) in enumerate(params):
        if i % 2 == 0:
            x, s = _gla_short_call(x, gla_s[i // 2], mix, n_tok=n_tok)
            gla_new.append(s)
        else:
            x, tail = _sc_short_call(x, _short_inject(conv_c[i // 2], n_tok), mix,
                                     n_seg=n_seg, n_tok=n_tok)
            conv_new.append(keep(tail))
        x, tail = _ffn_short_call(x, _short_inject(ffn_c[i], n_tok), ffn, gfin, n_seg=n_seg,
                                  n_tok=n_tok, final_norm=(i == depth - 1))
        ffn_new.append(keep(tail))
    y = x.reshape(n_seg, SHORT_STRIDE, d)[:, SHORT_STRIDE - n_tok:, :]
    return y, gla_new, conv_new, ffn_new


def _row(v):
    return v.reshape(1, -1).astype(_F32)


def _prep_gla(norm, w_in, w_gate_up, b_gate, head_gain, w_out):
    main = w_in.shape[1] - GLA_GATE_RANK
    wgd = jnp.pad(w_in[:, main:], ((0, 0), (0, LANES - GLA_GATE_RANK)))
    wgu = jnp.pad(w_gate_up, ((0, LANES - GLA_GATE_RANK), (0, 0)))
    return dict(gn=_row(norm), win=w_in[:, :main].astype(_BF16), wgd=wgd.astype(_BF16),
                wgu=wgu.astype(_BF16), bgate=_row(b_gate), hg=_row(head_gain),
                wout=w_out.astype(_BF16))


def _prep_sc(norm, w_in, conv_w, w_out):
    return dict(gn=_row(norm), win=w_in.astype(_BF16), cw=conv_w.astype(_F32),
                wout=w_out.astype(_BF16))


def _prep_ffn(norm, w_up, conv_w, conv_b, w_down):
    assert w_down.shape[0] % FFN_COLS == 0
    return dict(gn=_row(norm), wup=w_up.astype(_BF16), cw=conv_w.astype(_F32), cb=_row(conv_b),
                wd=w_down.astype(_BF16))


def _tail_from_cache(cache):
    return jnp.pad(cache, ((0, 0), (SUBLANES - (CONV_W - 1), 0), (0, 0)))


def _ffn_tail_from_cache(cache, n_c, fc):
    b = cache.shape[0]
    t = _tail_from_cache(cache).reshape(b, SUBLANES, 2, n_c, fc)
    return t.transpose(0, 2, 3, 1, 4)


def _ffn_cache_from_tail(tail):
    b = tail.shape[0]
    rows = tail[:, :, :, SUBLANES - (CONV_W - 1):, :]
    return rows.transpose(0, 3, 1, 2, 4).reshape(b, CONV_W - 1, -1)


def _trunk(x, gla_s, conv_tail, ffn_tail, params, gfin, *, tm_gla, tm_sc, tm_ffn, chunk):
    n_seq = x.shape[0]
    depth = len(params)
    gla_new, conv_new, ffn_new = [], [], []
    i_gla = i_conv = 0
    for i, (mix, ffn) in enumerate(params):
        if i % 2 == 0:
            x, s = _gla_call(x, gla_s[i_gla], mix, n_seq=n_seq, tm=tm_gla, chunk=chunk)
            gla_new.append(s)
            i_gla += 1
        else:
            x, s = _sc_call(x, conv_tail[i_conv], mix, n_seq=n_seq, tm=tm_sc)
            conv_new.append(s)
            i_conv += 1
        x, s = _ffn_call(x, ffn_tail[i], ffn, gfin, n_seq=n_seq, tm=tm_ffn,
                         final_norm=(i == depth - 1))
        ffn_new.append(s)
    return x, gla_new, conv_new, ffn_new


def kernel(x_prompt, x_sample, state_gla, cache_conv, cache_ffn, meta, norm_mix, norm_ffn, norm_final,
           gla_w_in, gla_w_gate_up, gla_b_gate, gla_head_gain, gla_w_out,
           sc_w_in, sc_conv_w, sc_w_out, ffn_w_up, ffn_conv_w, ffn_conv_b, ffn_w_down):
    depth = norm_mix.shape[0]
    n_b, seq, d = x_prompt.shape
    n_s, s_len, _ = x_sample.shape
    assert s_len == N_PREFIX and seq % GLA_CHUNK == 0
    f = ffn_w_down.shape[1]
    n_c, fc = f // FFN_COLS, FFN_COLS

    params = []
    for i in range(depth):
        j = i // 2
        if i % 2 == 0:
            mix = _prep_gla(norm_mix[i], gla_w_in[j], gla_w_gate_up[j], gla_b_gate[j],
                            gla_head_gain[j], gla_w_out[j])
        else:
            mix = _prep_sc(norm_mix[i], sc_w_in[j], sc_conv_w[j], sc_w_out[j])
        params.append((mix, _prep_ffn(norm_ffn[i], ffn_w_up[i], ffn_conv_w[i], ffn_conv_b[i],
                                      ffn_w_down[i])))
    gfin = _row(norm_final)

    xs = jnp.concatenate([x_sample, meta[None].astype(x_sample.dtype)], axis=0)
    zero1 = lambda a: jnp.zeros((a.shape[0], 1) + a.shape[2:], a.dtype)
    gla_s = jnp.concatenate([state_gla, zero1(state_gla)], axis=1).astype(_F32)
    conv_c = jnp.concatenate([cache_conv, zero1(cache_conv)], axis=1).astype(_F32)
    ffn_c = jnp.concatenate([cache_ffn, zero1(cache_ffn)], axis=1).astype(_F32)
    ys, gla_1, conv_1, ffn_1 = _short_trunk(xs, gla_s, conv_c, ffn_c, params, gfin)

    yp, gla_2, conv_2, ffn_2 = _trunk(
        x_prompt, [s[n_s:] for s in gla_1],
        [_tail_from_cache(s[n_s:]) for s in conv_1],
        [_ffn_tail_from_cache(s[n_s:], n_c, fc) for s in ffn_1],
        params, gfin, tm_gla=min(TILE_GLA, seq), tm_sc=min(TILE_SC, seq), tm_ffn=min(TILE_FFN, seq),
        chunk=GLA_CHUNK)

    dt = x_prompt.dtype
    tail2 = lambda s: s[:, SUBLANES - (CONV_W - 1):, :]
    return (yp.astype(dt), ys[:n_s].astype(dt),
            jnp.stack(gla_2).astype(dt), jnp.stack([s[:n_s] for s in gla_1]).astype(dt),
            jnp.stack([tail2(s) for s in conv_2]).astype(dt),
            jnp.stack([s[:n_s] for s in conv_1]).astype(dt),
            jnp.stack([_ffn_cache_from_tail(s) for s in ffn_2]).astype(dt),
            jnp.stack([s[:n_s] for s in ffn_1]).astype(dt))
```
